```python
import math
import jax, jax.numpy as jnp
from jax import lax
import numpy as np

D_MODEL = 1024
BATCH = 32
SEQ = 256
DEPTH = 4
DEC_BATCH = 4
DEC_SEQ = 4096
PAST_LEN = 512

GRID_W = 64
N_MIXERS = 3
N_SC = len(range(0, DEPTH, N_MIXERS))
N_MLA = len(range(1, DEPTH, N_MIXERS))
N_HY = len(range(2, DEPTH, N_MIXERS))
D_FF = 4 * D_MODEL
MOD_CHUNKS = 6
EPS = 1e-6
N_HEADS = 16
QK_NOPE = 64
ROPE_DIM = 32
AXIS_ROPE = ROPE_DIM // 2
QK_DIM = QK_NOPE + ROPE_DIM
V_DIM = 64
Q_RANK = 384
KV_RANK = 256
ROPE_THETA = 10000.0
Q_BLOCK = 128
HY_EMB = 33
HY_BANDS = (HY_EMB - 1) // 2
HY_ORDER = 64
HY_TARGET = 1e-2
HY_FAST = 0.3
HY_SLOW = 1.5

kernel_name = 'hybrid_conv_mla_hyena_diffusion_step'

F32 = jnp.float32


def rmsnorm(x, g):
    xf = x.astype(F32)
    y = xf * lax.rsqrt(jnp.mean(xf * xf, axis=-1, keepdims=True) + EPS)
    return (y * g.astype(F32)).astype(x.dtype)


def dwconv3(u, w, b):
    up = jnp.pad(u, ((0, 0), (1, 1), (0, 0)))
    return up[:, :-2] * w[0] + up[:, 1:-1] * w[1] + up[:, 2:] * w[2] + b


def short_conv_mixer(h, w_in, conv_w, conv_b, w_out):
    bg, cg, xv = jnp.split(h @ w_in, 3, axis=-1)
    u = dwconv3(cg * xv, conv_w, conv_b)
    return (bg * u) @ w_out


def hyena_filter(n, w0, b0, w1, b1, w2, b2, w3, freq):
    t01 = jnp.linspace(0.0, 1.0, n, dtype=F32)[:, None]
    w = 2.0 * math.pi * jnp.arange(n, dtype=F32)[:, None] / n
    fb = jnp.linspace(1e-4, HY_BANDS - 1, HY_BANDS, dtype=F32)[None, :]
    z = jnp.concatenate([t01, jnp.cos(fb * w), -jnp.sin(fb * w)], axis=-1)
    fr = freq.astype(F32)
    a = jnp.sin(fr * (z @ w0.astype(F32) + b0.astype(F32)))
    a = jnp.sin(fr * (a @ w1.astype(F32) + b1.astype(F32)))
    a = jnp.sin(fr * (a @ w2.astype(F32) + b2.astype(F32)))
    hf, hb = jnp.split(a @ w3.astype(F32), 2, axis=-1)
    deltas = jnp.abs(jnp.linspace(math.log(HY_TARGET) / HY_SLOW, math.log(HY_TARGET) / HY_FAST,
                                  D_MODEL, dtype=F32))
    decay = jnp.exp(-t01 * deltas)
    hf = hf * decay
    hb = hb * decay
    filt = jnp.concatenate([hf, jnp.zeros((1, D_MODEL), F32), hb[1:][::-1]], axis=0)
    return filt / jnp.sum(jnp.abs(filt), axis=0, keepdims=True)


def long_conv(v, filt):
    n = v.shape[1]
    vf = jnp.fft.rfft(v.astype(F32), n=2 * n, axis=1)
    ff = jnp.fft.rfft(filt, axis=0)
    y = jnp.fft.irfft(vf * ff[None], n=2 * n, axis=1)[:, :n]
    return y.astype(v.dtype)


def hyena_mixer(h, w_in, conv_w, conv_b, w0, b0, w1, b1, w2, b2, w3, freq, bias, w_out):
    n = h.shape[1]
    x0, x1, v = jnp.split(dwconv3(h @ w_in, conv_w, conv_b), 3, axis=-1)
    filt = hyena_filter(n, w0, b0, w1, b1, w2, b2, w3, freq)
    v = v * x1
    v = long_conv(v, filt) + v * bias
    return (v * x0) @ w_out


def grid_positions(n):
    rows = n // GRID_W
    row = jnp.repeat(jnp.arange(rows), GRID_W)
    col = jnp.tile(jnp.arange(GRID_W), rows)
    return row, col


def axial_rope(x, row, col):
    inv = jnp.power(ROPE_THETA, -jnp.arange(0, AXIS_ROPE, 2, dtype=F32) / AXIS_ROPE)

    def rot(xa, pos):
        ang = pos.astype(F32)[:, None] * inv[None]
        cos = jnp.cos(ang)[None, :, None, :].astype(xa.dtype)
        sin = jnp.sin(ang)[None, :, None, :].astype(xa.dtype)
        x1, x2 = jnp.split(xa, 2, axis=-1)
        return jnp.concatenate([x1 * cos - x2 * sin, x2 * cos + x1 * sin], axis=-1)

    xr, xc = jnp.split(x, 2, axis=-1)
    return jnp.concatenate([rot(xr, row), rot(xc, col)], axis=-1)


def rope_tail(x, row, col):
    return jnp.concatenate([x[..., :QK_NOPE], axial_rope(x[..., QK_NOPE:], row, col)], axis=-1)


def mla_queries(h, w_dq, q_norm_g, w_uq, qn_g):
    b, n, _ = h.shape
    q = (rmsnorm(h @ w_dq, q_norm_g) @ w_uq).reshape(b, n, N_HEADS, QK_DIM)
    return rmsnorm(q, qn_g)


def mla_latent_kv(h, w_dkv, kv_norm_g):
    ckv, kpe = jnp.split(h @ w_dkv, [KV_RANK], axis=-1)
    return rmsnorm(ckv, kv_norm_g), kpe


def mla_expand(ckv, kpe, w_ukv, kn_g):
    b, n, _ = ckv.shape
    kn, v = jnp.split((ckv @ w_ukv).reshape(b, n, N_HEADS, QK_NOPE + V_DIM), [QK_NOPE], axis=-1)
    k = jnp.concatenate([kn, jnp.broadcast_to(kpe[:, :, None, :], (b, n, N_HEADS, ROPE_DIM))], axis=-1)
    return rmsnorm(k, kn_g), v


def block_attention(q, k, v):
    b, n, h, dk = q.shape
    nb = n // Q_BLOCK
    qb = jnp.moveaxis(q.reshape(b, nb, Q_BLOCK, h, dk), 1, 0)
    scale = dk ** -0.5

    def one(qblk):
        s = jnp.einsum('bqhd,bkhd->bhqk', qblk, k, preferred_element_type=F32) * scale
        p = jax.nn.softmax(s, axis=-1).astype(v.dtype)
        return jnp.einsum('bhqk,bkhd->bqhd', p, v)

    o = lax.map(one, qb)
    return jnp.moveaxis(o, 0, 1).reshape(b, n, h * v.shape[-1])


def mla_context(h, P, j):
    q = mla_queries(h, P['mla_w_dq'][j], P['mla_q_norm_g'][j], P['mla_w_uq'][j], P['mla_qn_g'][j])
    ckv, kpe = mla_latent_kv(h, P['mla_w_dkv'][j], P['mla_kv_norm_g'][j])
    k, v = mla_expand(ckv, kpe, P['mla_w_ukv'][j], P['mla_kn_g'][j])
    return block_attention(q, k, v) @ P['mla_w_o'][j], ckv, kpe


def mla_latent(h, ctx_ckv, ctx_kpe, P, j):
    row, col = grid_positions(h.shape[1])
    q = rope_tail(mla_queries(h, P['mla_w_dq'][j], P['mla_q_norm_g'][j], P['mla_w_uq'][j],
                              P['mla_qn_g'][j]), row, col)
    ckv, kpe = mla_latent_kv(h, P['mla_w_dkv'][j], P['mla_kv_norm_g'][j])
    k, v = mla_expand(ckv, kpe, P['mla_w_ukv'][j], P['mla_kn_g'][j])
    k = rope_tail(k, row, col)
    kc, vc = mla_expand(ctx_ckv, ctx_kpe, P['mla_w_ukv'][j], P['mla_kn_g'][j])
    o = block_attention(q, jnp.concatenate([k, kc], axis=1), jnp.concatenate([v, vc], axis=1))
    return o @ P['mla_w_o'][j]


def trunk(x, cond, P, cache_ckv=None, cache_kpe=None):
    is_context = cache_ckv is None
    ckvs, kpes = [], []
    for i in range(DEPTH):
        kind, j = i % N_MIXERS, i // N_MIXERS
        sh1, sc1, g1, sh2, sc2, g2 = jnp.split(jax.nn.silu(cond) @ P['mod_w'][i] + P['mod_b'][i],
                                               MOD_CHUNKS, axis=-1)
        h = rmsnorm(x, P['norm1_g'][i]) * (1 + sc1) + sh1
        if kind == 0:
            y = short_conv_mixer(h, P['sc_w_in'][j], P['sc_conv_w'][j], P['sc_conv_b'][j], P['sc_w_out'][j])
        elif kind == 1:
            if is_context:
                y, ckv, kpe = mla_context(h, P, j)
                ckvs.append(ckv)
                kpes.append(kpe)
            else:
                y = mla_latent(h, cache_ckv[:, j], cache_kpe[:, j], P, j)
        else:
            y = hyena_mixer(h, P['hy_w_in'][j], P['hy_conv_w'][j], P['hy_conv_b'][j],
                            P['hy_f_w0'][j], P['hy_f_b0'][j], P['hy_f_w1'][j], P['hy_f_b1'][j],
                            P['hy_f_w2'][j], P['hy_f_b2'][j], P['hy_f_w3'][j], P['hy_sin_freq'][j],
                            P['hy_bias'][j], P['hy_w_out'][j])
        x = x + g1 * y
        h = rmsnorm(x, P['norm2_g'][i]) * (1 + sc2) + sh2
        x = x + g2 * (jnp.square(jax.nn.relu(h @ P['mlp_w1'][i])) @ P['mlp_w2'][i])
    return x, ckvs, kpes


def setup_inputs(seed: int = 0) -> dict:
    key = jax.random.key(seed)
    ks = iter(jax.random.split(key, 64))

    def nrm(shape, scale):
        return scale * jax.random.normal(next(ks), shape, F32)

    def gain(shape):
        return 1.0 + nrm(shape, 0.02)

    D = D_MODEL
    return {
        'x_prompt': nrm((BATCH, SEQ, D), 1.0),
        'x_sample': nrm((DEC_BATCH, DEC_SEQ, D), 1.0),
        'cache_ckv': nrm((DEC_BATCH, N_MLA, PAST_LEN, KV_RANK), 1.0),
        'cache_kpe': nrm((DEC_BATCH, N_MLA, PAST_LEN, ROPE_DIM), 1.0),
        'c': nrm((DEC_BATCH, D), 1.0),
        'c_ctx': nrm((D,), 1.0),
        'norm1_g': gain((DEPTH, D)),
        'norm2_g': gain((DEPTH, D)),
        'mod_w': nrm((DEPTH, D, MOD_CHUNKS * D), 0.5 * D ** -0.5),
        'mod_b': nrm((DEPTH, MOD_CHUNKS * D), 0.02),
        'mlp_w1': nrm((DEPTH, D, D_FF), D ** -0.5),
        'mlp_w2': nrm((DEPTH, D_FF, D), D_FF ** -0.5),
        'sc_w_in': nrm((N_SC, D, 3 * D), D ** -0.5),
        'sc_conv_w': nrm((N_SC, 3, D), 3 ** -0.5),
        'sc_conv_b': nrm((N_SC, D), 0.02),
        'sc_w_out': nrm((N_SC, D, D), D ** -0.5),
        'mla_w_dq': nrm((N_MLA, D, Q_RANK), D ** -0.5),
        'mla_q_norm_g': gain((N_MLA, Q_RANK)),
        'mla_w_uq': nrm((N_MLA, Q_RANK, N_HEADS * QK_DIM), Q_RANK ** -0.5),
        'mla_w_dkv': nrm((N_MLA, D, KV_RANK + ROPE_DIM), D ** -0.5),
        'mla_kv_norm_g': gain((N_MLA, KV_RANK)),
        'mla_w_ukv': nrm((N_MLA, KV_RANK, N_HEADS * (QK_NOPE + V_DIM)), KV_RANK ** -0.5),
        'mla_qn_g': gain((N_MLA, QK_DIM)),
        'mla_kn_g': gain((N_MLA, QK_DIM)),
        'mla_w_o': nrm((N_MLA, N_HEADS * V_DIM, D), (N_HEADS * V_DIM) ** -0.5),
        'hy_w_in': nrm((N_HY, D, 3 * D), D ** -0.5),
        'hy_conv_w': nrm((N_HY, 3, 3 * D), 3 ** -0.5),
        'hy_conv_b': nrm((N_HY, 3 * D), 0.02),
        'hy_f_w0': nrm((N_HY, HY_EMB, HY_ORDER), HY_EMB ** -0.5),
        'hy_f_b0': nrm((N_HY, HY_ORDER), 0.02),
        'hy_f_w1': nrm((N_HY, HY_ORDER, HY_ORDER), HY_ORDER ** -0.5),
        'hy_f_b1': nrm((N_HY, HY_ORDER), 0.02),
        'hy_f_w2': nrm((N_HY, HY_ORDER, HY_ORDER), HY_ORDER ** -0.5),
        'hy_f_b2': nrm((N_HY, HY_ORDER), 0.02),
        'hy_f_w3': nrm((N_HY, HY_ORDER, 2 * D), HY_ORDER ** -0.5),
        'hy_sin_freq': gain((N_HY, HY_ORDER)),
        'hy_bias': nrm((N_HY, D), 0.1),
        'hy_w_out': nrm((N_HY, D, D), D ** -0.5),
    }


def reference(x_prompt, x_sample, cache_ckv, cache_kpe, c, c_ctx, norm1_g, norm2_g, mod_w, mod_b,
              mlp_w1, mlp_w2, sc_w_in, sc_conv_w, sc_conv_b, sc_w_out, mla_w_dq, mla_q_norm_g,
              mla_w_uq, mla_w_dkv, mla_kv_norm_g, mla_w_ukv, mla_qn_g, mla_kn_g, mla_w_o, hy_w_in,
              hy_conv_w, hy_conv_b, hy_f_w0, hy_f_b0, hy_f_w1, hy_f_b1, hy_f_w2, hy_f_b2, hy_f_w3,
              hy_sin_freq, hy_bias, hy_w_out):
    P = dict(norm1_g=norm1_g, norm2_g=norm2_g, mod_w=mod_w, mod_b=mod_b, mlp_w1=mlp_w1,
             mlp_w2=mlp_w2, sc_w_in=sc_w_in, sc_conv_w=sc_conv_w, sc_conv_b=sc_conv_b,
             sc_w_out=sc_w_out, mla_w_dq=mla_w_dq, mla_q_norm_g=mla_q_norm_g, mla_w_uq=mla_w_uq,
             mla_w_dkv=mla_w_dkv, mla_kv_norm_g=mla_kv_norm_g, mla_w_ukv=mla_w_ukv,
             mla_qn_g=mla_qn_g, mla_kn_g=mla_kn_g, mla_w_o=mla_w_o, hy_w_in=hy_w_in,
             hy_conv_w=hy_conv_w, hy_conv_b=hy_conv_b, hy_f_w0=hy_f_w0, hy_f_b0=hy_f_b0,
             hy_f_w1=hy_f_w1, hy_f_b1=hy_f_b1, hy_f_w2=hy_f_w2, hy_f_b2=hy_f_b2, hy_f_w3=hy_f_w3,
             hy_sin_freq=hy_sin_freq, hy_bias=hy_bias, hy_w_out=hy_w_out)
    y_prompt, ckvs, kpes = trunk(x_prompt, c_ctx[None, None, :], P)
    new_ckv = jnp.stack(ckvs, axis=1)
    new_kpe = jnp.stack(kpes, axis=1)
    y_sample, _, _ = trunk(x_sample, c[:, None, :], P, cache_ckv, cache_kpe)
    return (y_prompt, y_sample, new_ckv, new_kpe)
```

```python
import functools
import math

import numpy as np
import jax
import jax.numpy as jnp
from jax import lax
from jax.experimental import pallas as pl
from jax.experimental.pallas import tpu as pltpu

F32 = jnp.float32
BF16 = jnp.bfloat16

EPS = 1e-6
MOD_CHUNKS = 6
N_HEADS = 16
QK_NOPE = 64
ROPE_DIM = 32
QK_DIM = QK_NOPE + ROPE_DIM
V_DIM = 64
HEAD_SLOT = 128
GRID_W = 64
ROPE_THETA = 10000.0
HY_TARGET = 1e-2
HY_FAST = 0.3
HY_SLOW = 1.5
HY_BLOCK = 512
NYQ_ROWS = 16
MOD_ROWS = 8

V7X_VMEM_BYTES = 64 * 1024 * 1024
VMEM_LIMIT = V7X_VMEM_BYTES - 8 * 1024 * 1024

SH1, SC1, G1, SH2, SC2, G2 = range(6)


def _tile(n, pref, align=128):
    if n <= pref:
        return n
    t = pref - pref % align
    while n % t:
        t -= align
    return t


def _cparams(n_axes):
    return pltpu.CompilerParams(dimension_semantics=("arbitrary",) * n_axes,
                                vmem_limit_bytes=VMEM_LIMIT)


def _mod_chunk(mod_ref, r, k, d):
    return mod_ref[pl.ds(r, 1), k * d:(k + 1) * d]


def _mod_row(i, row_base, tiles_per_batch):
    if tiles_per_batch is None:
        return row_base
    return row_base + i // tiles_per_batch


def _modnorm(x, g, sc, sh):
    ms = jnp.mean(x * x, axis=-1, keepdims=True)
    return x * lax.rsqrt(ms + EPS) * (g * (1.0 + sc)) + sh


def _rmsnorm(x, g):
    ms = jnp.mean(x * x, axis=-1, keepdims=True)
    return x * lax.rsqrt(ms + EPS) * g


def _dot(a, b):
    return jnp.dot(a, b, preferred_element_type=F32)


def _mod_kernel(c_ref, w_ref, b_ref, o_ref):
    c = c_ref[...]
    s = (c * jax.nn.sigmoid(c)).astype(BF16)
    o_ref[0] = _dot(s, w_ref[0].astype(BF16)) + b_ref[0]


def _modulation(cond, mod_w, mod_b):
    depth, d, n6 = mod_w.shape
    tn = _tile(n6, 1536)
    return pl.pallas_call(
        _mod_kernel,
        grid=(depth, n6 // tn),
        in_specs=[pl.BlockSpec((MOD_ROWS, d), lambda l, j: (0, 0)),
                  pl.BlockSpec((1, d, tn), lambda l, j: (l, 0, j)),
                  pl.BlockSpec((1, 1, tn), lambda l, j: (l, 0, j))],
        out_specs=pl.BlockSpec((1, MOD_ROWS, tn), lambda l, j: (l, 0, j)),
        out_shape=jax.ShapeDtypeStruct((depth, MOD_ROWS, n6), F32),
        compiler_params=_cparams(2),
        name="modulation",
    )(cond, mod_w, mod_b.reshape(depth, 1, n6))


def _mlp_kernel(x_ref, mod_ref, g_ref, w1_ref, w2_ref, o_ref, h_ref, acc_ref, *, row_base, tpb):
    i = pl.program_id(0)
    f = pl.program_id(1)
    d = x_ref.shape[1]
    r = _mod_row(i, row_base, tpb)

    @pl.when(f == 0)
    def _():
        h = _modnorm(x_ref[...], g_ref[...], _mod_chunk(mod_ref, r, SC2, d), _mod_chunk(mod_ref, r, SH2, d))
        h_ref[...] = h.astype(BF16)
        acc_ref[...] = jnp.zeros_like(acc_ref)

    a = jnp.maximum(_dot(h_ref[...], w1_ref[...]), 0.0)
    acc_ref[...] += _dot((a * a).astype(BF16), w2_ref[...])

    @pl.when(f == pl.num_programs(1) - 1)
    def _():
        o_ref[...] = x_ref[...] + _mod_chunk(mod_ref, r, G2, d) * acc_ref[...]


def _mlp(x, mod, g, w1, w2, *, row_base, rows_per_batch):
    n, d = x.shape
    dff = w1.shape[1]
    tm = min(1024, n if rows_per_batch is None else rows_per_batch)
    tf = _tile(dff, 512)
    tpb = None if rows_per_batch is None else rows_per_batch // tm
    return pl.pallas_call(
        functools.partial(_mlp_kernel, row_base=row_base, tpb=tpb),
        grid=(n // tm, dff // tf),
        in_specs=[pl.BlockSpec((tm, d), lambda i, f: (i, 0)),
                  pl.BlockSpec(mod.shape, lambda i, f: (0, 0)),
                  pl.BlockSpec((1, d), lambda i, f: (0, 0)),
                  pl.BlockSpec((d, tf), lambda i, f: (0, f)),
                  pl.BlockSpec((tf, d), lambda i, f: (f, 0))],
        out_specs=pl.BlockSpec((tm, d), lambda i, f: (i, 0)),
        out_shape=jax.ShapeDtypeStruct((n, d), F32),
        scratch_shapes=[pltpu.VMEM((tm, d), BF16), pltpu.VMEM((tm, d), F32)],
        compiler_params=_cparams(2),
        name="mlp",
    )(x, mod, g.reshape(1, d), w1, w2)


def _nm_matmul_kernel(x_ref, mod_ref, g_ref, w_ref, o_ref, h_ref, *, row_base, tpb):
    i = pl.program_id(0)
    d = x_ref.shape[1]

    @pl.when(pl.program_id(1) == 0)
    def _():
        r = _mod_row(i, row_base, tpb)
        h = _modnorm(x_ref[...], g_ref[...], _mod_chunk(mod_ref, r, SC1, d), _mod_chunk(mod_ref, r, SH1, d))
        h_ref[...] = h.astype(BF16)

    o_ref[...] = _dot(h_ref[...], w_ref[...]).astype(o_ref.dtype)


def _nm_matmul(x, mod, g, w, *, row_base, rows_per_batch):
    n, d = x.shape
    nout = w.shape[1]
    tm = min(1024, n if rows_per_batch is None else rows_per_batch)
    tn = _tile(nout, 1024)
    tpb = None if rows_per_batch is None else rows_per_batch // tm
    return pl.pallas_call(
        functools.partial(_nm_matmul_kernel, row_base=row_base, tpb=tpb),
        grid=(n // tm, nout // tn),
        in_specs=[pl.BlockSpec((tm, d), lambda i, j: (i, 0)),
                  pl.BlockSpec(mod.shape, lambda i, j: (0, 0)),
                  pl.BlockSpec((1, d), lambda i, j: (0, 0)),
                  pl.BlockSpec((d, tn), lambda i, j: (0, j))],
        out_specs=pl.BlockSpec((tm, tn), lambda i, j: (i, j)),
        out_shape=jax.ShapeDtypeStruct((n, nout), BF16),
        scratch_shapes=[pltpu.VMEM((tm, d), BF16)],
        compiler_params=_cparams(2),
        name="norm_proj",
    )(x, mod, g.reshape(1, d), w)


def _mm_res_kernel(z_ref, w_ref, x_ref, mod_ref, o_ref, *, row_base, tpb):
    d = x_ref.shape[1]
    r = _mod_row(pl.program_id(0), row_base, tpb)
    o_ref[...] = x_ref[...] + _mod_chunk(mod_ref, r, G1, d) * _dot(z_ref[...], w_ref[...])


def _mm_res(z, w, x, mod, *, row_base, rows_per_batch):
    n, d = x.shape
    k = z.shape[1]
    tm = min(1024, n if rows_per_batch is None else rows_per_batch)
    tpb = None if rows_per_batch is None else rows_per_batch // tm
    return pl.pallas_call(
        functools.partial(_mm_res_kernel, row_base=row_base, tpb=tpb),
        grid=(n // tm,),
        in_specs=[pl.BlockSpec((tm, k), lambda i: (i, 0)),
                  pl.BlockSpec((k, d), lambda i: (0, 0)),
                  pl.BlockSpec((tm, d), lambda i: (i, 0)),
                  pl.BlockSpec(mod.shape, lambda i: (0, 0))],
        out_specs=pl.BlockSpec((tm, d), lambda i: (i, 0)),
        out_shape=jax.ShapeDtypeStruct((n, d), F32),
        compiler_params=_cparams(1),
        name="out_proj_res",
    )(z, w, x, mod)


HALO = 16


def _sconv_out_kernel(bg_ref, cg_ref, xv_ref, cgp_ref, xvp_ref, cgn_ref, xvn_ref, cw_ref, cb_ref,
                      w_ref, x_ref, mod_ref, o_ref, *, seq_len, row_base, tpb):
    i = pl.program_id(0)
    tm, d = x_ref.shape
    cx = cg_ref[...].astype(F32) * xv_ref[...].astype(F32)
    prev = (cgp_ref[...].astype(F32) * xvp_ref[...].astype(F32))[HALO - 1:HALO, :]
    nxt = (cgn_ref[...].astype(F32) * xvn_ref[...].astype(F32))[0:1, :]
    row = lax.broadcasted_iota(jnp.int32, (tm, 1), 0)
    pos = jnp.bitwise_and(i * tm + row, seq_len - 1)
    dn = jnp.where(row == 0, prev, pltpu.roll(cx, 1, 0))
    dn = jnp.where(pos == 0, 0.0, dn)
    up = jnp.where(row == tm - 1, nxt, pltpu.roll(cx, tm - 1, 0))
    up = jnp.where(pos == seq_len - 1, 0.0, up)
    cw = cw_ref[...]
    u = dn * cw[0:1, :] + cx * cw[1:2, :] + up * cw[2:3, :] + cb_ref[...]
    z = (bg_ref[...].astype(F32) * u).astype(BF16)
    r = _mod_row(i, row_base, tpb)
    o_ref[...] = x_ref[...] + _mod_chunk(mod_ref, r, G1, d) * _dot(z, w_ref[...])


def _sconv_out(p, conv_w, conv_b, w_out, x, mod, *, seq_len, row_base, rows_per_batch):
    n, d = x.shape
    assert seq_len & (seq_len - 1) == 0
    tm = min(1024, n if rows_per_batch is None else rows_per_batch)
    tpb = None if rows_per_batch is None else rows_per_batch // tm
    hb = tm // HALO
    last = n // HALO - 1
    main = lambda c: pl.BlockSpec((tm, d), lambda i: (i, c))
    prev = lambda c: pl.BlockSpec((HALO, d), lambda i: (jnp.maximum(i * hb - 1, 0), c))
    nxt = lambda c: pl.BlockSpec((HALO, d), lambda i: (jnp.minimum((i + 1) * hb, last), c))
    return pl.pallas_call(
        functools.partial(_sconv_out_kernel, seq_len=seq_len, row_base=row_base, tpb=tpb),
        grid=(n // tm,),
        in_specs=[main(0), main(1), main(2), prev(1), prev(2), nxt(1), nxt(2),
                  pl.BlockSpec((3, d), lambda i: (0, 0)),
                  pl.BlockSpec((1, d), lambda i: (0, 0)),
                  pl.BlockSpec((d, d), lambda i: (0, 0)),
                  pl.BlockSpec((tm, d), lambda i: (i, 0)),
                  pl.BlockSpec(mod.shape, lambda i: (0, 0))],
        out_specs=pl.BlockSpec((tm, d), lambda i: (i, 0)),
        out_shape=jax.ShapeDtypeStruct((n, d), F32),
        compiler_params=_cparams(1),
        name="sconv_out",
    )(p, p, p, p, p, p, p, conv_w, conv_b.reshape(1, d), w_out, x, mod)


def _dft_constants(m):
    k = np.arange(m)[:, None]
    s = np.arange(m)[None, :]
    ang = np.pi * ((k * s) % (2 * m)) / m
    rows = 2 * m + NYQ_ROWS
    ma = np.zeros((rows, m))
    ma[:m] = np.cos(ang)
    ma[m:2 * m] = -np.sin(ang)
    ma[2 * m] = (-1.0) ** np.arange(m)
    sign = np.ones((rows, 1))
    sign[:m, 0] = (-1.0) ** np.arange(m)
    sign[m:2 * m, 0] = (-1.0) ** np.arange(m)
    mb = ma * sign
    mb[:, 0] = 0.0
    mab = np.concatenate([mb, ma], axis=1)
    t = np.arange(m)[:, None]
    kk = np.arange(m)[None, :]
    ang_i = np.pi * ((t * kk) % (2 * m)) / m
    ck = np.full((1, m), 2.0)
    ck[0, 0] = 1.0
    minv = np.concatenate([ck * np.cos(ang_i), -ck * np.sin(ang_i)], axis=1) / (2 * m)
    return (jnp.asarray(ma, F32).astype(BF16), jnp.asarray(mab, F32).astype(BF16),
            jnp.asarray(minv, F32).astype(BF16))


def _filter_features(n, n_emb_pad):
    bands = (33 - 1) // 2
    t01 = np.linspace(0.0, 1.0, n)[:, None]
    w = 2.0 * np.pi * np.arange(n)[:, None] / n
    fb = np.linspace(1e-4, bands - 1, bands)[None, :]
    z = np.concatenate([t01, np.cos(fb * w), -np.sin(fb * w)], axis=-1)
    zr = np.zeros_like(z)
    zr[1:] = z[1:][::-1]
    zf = np.concatenate([zr, z], axis=0)
    out = np.zeros((2 * n, n_emb_pad))
    out[:, :z.shape[1]] = zf
    return jnp.asarray(out, F32)


def _hy_filter_kernel(z_ref, w0_ref, b0_ref, w1_ref, b1_ref, w2_ref, b2_ref, fr_ref, w3b_ref, w3f_ref,
                      dl_ref, mab_ref, g_ref, a3_ref, f_ref, *, n, m):
    c = pl.program_id(0)
    dstep = pl.program_id(1)
    ct = f_ref.shape[1]

    @pl.when(jnp.logical_and(c == 0, dstep == 0))
    def _():
        fr = fr_ref[...]
        a = jnp.sin(fr * (_dot(z_ref[...].astype(BF16), w0_ref[...]) + b0_ref[...]))
        a = jnp.sin(fr * (_dot(a.astype(BF16), w1_ref[...]) + b1_ref[...]))
        a = jnp.sin(fr * (_dot(a.astype(BF16), w2_ref[...]) + b2_ref[...]))
        a3_ref[...] = a.astype(BF16)

    @pl.when(dstep == 0)
    def _():
        q = lax.broadcasted_iota(jnp.int32, (n, 1), 0)
        qf = q.astype(F32)
        dl = dl_ref[...]
        t_fwd = qf / (n - 1.0)
        t_bwd = (n - qf) / (n - 1.0)
        hb = _dot(a3_ref[0:n, :], w3b_ref[...]) * jnp.exp(-t_bwd * dl)
        hb = jnp.where(q == 0, 0.0, hb)
        hf = _dot(a3_ref[n:2 * n, :], w3f_ref[...]) * jnp.exp(-t_fwd * dl)
        norm = (jnp.sum(jnp.abs(hb), axis=0, keepdims=True)
                + jnp.sum(jnp.abs(hf), axis=0, keepdims=True))
        f_ref[0:n, :] = (hb / norm).astype(BF16)
        f_ref[n:2 * n, :] = (hf / norm).astype(BF16)

    start = pl.multiple_of(dstep * m, m)
    g_ref[0] = _dot(mab_ref[...], f_ref[pl.ds(start, 2 * m), :]).astype(g_ref.dtype)


def _hy_filter(n, m, w0, b0, w1, b1, w2, b2, w3, freq, mab):
    d = w3.shape[1] // 2
    order = w1.shape[0]
    nb = n // m
    ct = _tile(d, 256)
    emb_pad = 128
    z = _filter_features(n, emb_pad)
    w0p = jnp.zeros((emb_pad, order), F32).at[:w0.shape[0]].set(w0).astype(BF16)
    lo, hi = math.log(HY_TARGET) / HY_SLOW, math.log(HY_TARGET) / HY_FAST
    deltas = jnp.asarray(np.abs(np.linspace(lo, hi, d))[None, :], F32)
    w3b = w3.astype(BF16)
    rows = 2 * m + NYQ_ROWS
    full = lambda shp: pl.BlockSpec(shp, lambda c, s: (0,) * len(shp))
    return pl.pallas_call(
        functools.partial(_hy_filter_kernel, n=n, m=m),
        grid=(d // ct, 2 * nb - 1),
        in_specs=[full((2 * n, emb_pad)), full((emb_pad, order)), full((1, order)),
                  full((order, order)), full((1, order)), full((order, order)), full((1, order)),
                  full((1, order)),
                  pl.BlockSpec((order, ct), lambda c, s: (0, d // ct + c)),
                  pl.BlockSpec((order, ct), lambda c, s: (0, c)),
                  pl.BlockSpec((1, ct), lambda c, s: (0, c)),
                  full((rows, 2 * m))],
        out_specs=pl.BlockSpec((1, rows, ct), lambda c, s: (s, 0, c)),
        out_shape=jax.ShapeDtypeStruct((2 * nb - 1, rows, d), BF16),
        scratch_shapes=[pltpu.VMEM((2 * n, order), BF16), pltpu.VMEM((2 * n, ct), BF16)],
        compiler_params=_cparams(2),
        name="hyena_filter",
    )(z, w0p, b0.reshape(1, order), w1.astype(BF16), b1.reshape(1, order), w2.astype(BF16),
      b2.reshape(1, order), freq.reshape(1, order), w3b, w3b, deltas, mab)


def _conv3_rows(x, w, b):
    n = x.shape[0]
    row = lax.broadcasted_iota(jnp.int32, (n, 1), 0)
    dn = jnp.where(row == 0, 0.0, pltpu.roll(x, 1, 0))
    up = jnp.where(row == n - 1, 0.0, pltpu.roll(x, n - 1, 0))
    return dn * w[0:1, :] + x * w[1:2, :] + up * w[2:3, :] + b


def _hy_conv_kernel(p0_ref, p1_ref, p2_ref, w0_ref, w1_ref, w2_ref, b0_ref, b1_ref, b2_ref, hb_ref,
                    g_ref, ma_ref, minv_ref, o_ref, x0_ref, v_ref, vf_ref, *, m, nb):
    i = pl.program_id(2)

    @pl.when(i == 0)
    def _():
        x0_ref[...] = _conv3_rows(p0_ref[...].astype(F32), w0_ref[...], b0_ref[...])
        x1 = _conv3_rows(p1_ref[...].astype(F32), w1_ref[...], b1_ref[...])
        v = _conv3_rows(p2_ref[...].astype(F32), w2_ref[...], b2_ref[...])
        v_ref[...] = (v * x1).astype(BF16)
        for j in range(nb):
            vf_ref[j] = _dot(ma_ref[...], v_ref[j * m:(j + 1) * m, :])

    ct = o_ref.shape[1]
    acc_r = jnp.zeros((m, ct), F32)
    acc_i = jnp.zeros((m, ct), F32)
    acc_n = jnp.zeros((NYQ_ROWS, ct), F32)
    for j in range(nb):
        g = g_ref[i - j + nb - 1].astype(F32)
        vr = vf_ref[j, 0:m, :]
        vi = vf_ref[j, m:2 * m, :]
        gr = g[0:m, :]
        gi = g[m:2 * m, :]
        acc_r += gr * vr - gi * vi
        acc_i += gr * vi + gi * vr
        acc_n += g[2 * m:, :] * vf_ref[j, 2 * m:, :]
    spec = jnp.concatenate([acc_r, acc_i], axis=0).astype(BF16)
    t = lax.broadcasted_iota(jnp.int32, (m, 1), 0)
    sgn = (1 - 2 * jnp.bitwise_and(t, 1)).astype(F32) * (0.5 / m)
    y = _dot(minv_ref[...], spec) + sgn * acc_n[0:1, :]
    start = pl.multiple_of(i * m, m)
    v1 = v_ref[pl.ds(start, m), :].astype(F32)
    o_ref[...] = ((y + v1 * hb_ref[...]) * x0_ref[pl.ds(start, m), :]).astype(o_ref.dtype)


def _hy_conv(p, conv_w, conv_b, hy_bias, g, ma, minv, *, n, m, batch):
    d = hy_bias.shape[0]
    nb = n // m
    ct = _tile(d, 256)
    nct = d // ct
    rows = 2 * m + NYQ_ROWS
    pcol = lambda k: pl.BlockSpec((n, ct), lambda c, b, i: (b, k * nct + c))
    wcol = lambda k: pl.BlockSpec((3, ct), lambda c, b, i: (0, k * nct + c))
    bcol = lambda k: pl.BlockSpec((1, ct), lambda c, b, i: (0, k * nct + c))
    return pl.pallas_call(
        functools.partial(_hy_conv_kernel, m=m, nb=nb),
        grid=(nct, batch, nb),
        in_specs=[pcol(0), pcol(1), pcol(2), wcol(0), wcol(1), wcol(2), bcol(0), bcol(1), bcol(2),
                  pl.BlockSpec((1, ct), lambda c, b, i: (0, c)),
                  pl.BlockSpec((2 * nb - 1, rows, ct), lambda c, b, i: (0, 0, c)),
                  pl.BlockSpec((rows, m), lambda c, b, i: (0, 0)),
                  pl.BlockSpec((m, 2 * m), lambda c, b, i: (0, 0))],
        out_specs=pl.BlockSpec((m, ct), lambda c, b, i: (b * nb + i, c)),
        out_shape=jax.ShapeDtypeStruct((batch * n, d), BF16),
        scratch_shapes=[pltpu.VMEM((n, ct), F32), pltpu.VMEM((n, ct), BF16),
                        pltpu.VMEM((nb, rows, ct), F32)],
        compiler_params=_cparams(3),
        name="hyena_conv",
    )(p, p, p, conv_w, conv_w, conv_w, conv_b.reshape(1, 3 * d), conv_b.reshape(1, 3 * d),
      conv_b.reshape(1, 3 * d), hy_bias.reshape(1, d), g, ma, minv)


def _rope_tables(n):
    axis = ROPE_DIM // 2
    half = axis // 2
    inv = ROPE_THETA ** (-np.arange(0, axis, 2) / axis)
    t = np.arange(n)
    row, col = t // GRID_W, t % GRID_W
    cos = np.ones((n, HEAD_SLOT))
    sin_a = np.zeros((n, HEAD_SLOT))
    sin_b = np.zeros((n, HEAD_SLOT))
    for k, pos in enumerate((row, col)):
        ang = pos[:, None] * inv[None, :]
        base = QK_NOPE + k * axis
        cos[:, base:base + half] = np.cos(ang)
        cos[:, base + half:base + axis] = np.cos(ang)
        sin_a[:, base:base + half] = -np.sin(ang)
        sin_b[:, base + half:base + axis] = np.sin(ang)
    return jnp.asarray(cos, F32), jnp.asarray(sin_a, F32), jnp.asarray(sin_b, F32)


def _head_norm_rope(x, g, rope):
    ss = jnp.sum(x * x, axis=-1, keepdims=True)
    x = x * lax.rsqrt(ss * (1.0 / QK_DIM) + EPS) * g
    if rope is not None:
        cos, sin_a, sin_b = rope
        half = ROPE_DIM // 4
        x = x * cos + pltpu.roll(x, HEAD_SLOT - half, 1) * sin_a + pltpu.roll(x, half, 1) * sin_b
    return x


def _expand_keys(ckv_b, kpe_slot, wukv_ref, kng, rope, k_ref, v_ref):
    kw = N_HEADS * HEAD_SLOT
    kv = _dot(ckv_b, wukv_ref[...])
    for h in range(N_HEADS):
        sl = slice(h * HEAD_SLOT, (h + 1) * HEAD_SLOT)
        k_ref[:, sl] = _head_norm_rope(kv[:, sl] + kpe_slot, kng, rope).astype(BF16)
    v_ref[...] = kv[:, kw:].astype(BF16)


def _mla_prep_kernel(*refs, row_base, tpb, use_rope, emit_latent):
    (x_ref, mod_ref, g_ref, wcat_ref, qg_ref, kvg_ref, wuq_ref, wukv_ref, qng_ref, kng_ref), refs = refs[:10], refs[10:]
    if use_rope:
        (cos_ref, sa_ref, sb_ref), refs = refs[:3], refs[3:]
        rope = (cos_ref[...], sa_ref[...], sb_ref[...])
    else:
        rope = None
    q_ref, k_ref, v_ref = refs[:3]
    i = pl.program_id(0)
    d = x_ref.shape[1]
    r = _mod_row(i, row_base, tpb)
    h = _modnorm(x_ref[...], g_ref[...], _mod_chunk(mod_ref, r, SC1, d), _mod_chunk(mod_ref, r, SH1, d))
    a = _dot(h.astype(BF16), wcat_ref[...])
    q_rank = qg_ref.shape[1]
    kv_rank = kvg_ref.shape[1]
    qc = _rmsnorm(a[:, :q_rank], qg_ref[...]).astype(BF16)
    ckv = _rmsnorm(a[:, q_rank:q_rank + kv_rank], kvg_ref[...])
    kpe_slot = a[:, q_rank + kv_rank:]
    if emit_latent:
        refs[3][...] = ckv
        refs[4][...] = kpe_slot
    q = _dot(qc, wuq_ref[...])
    qng = qng_ref[...] * (QK_DIM ** -0.5 * math.log2(math.e))
    for hd in range(N_HEADS):
        sl = slice(hd * HEAD_SLOT, (hd + 1) * HEAD_SLOT)
        q_ref[:, sl] = _head_norm_rope(q[:, sl], qng, rope).astype(BF16)
    _expand_keys(ckv.astype(BF16), kpe_slot, wukv_ref, kng_ref[...], rope, k_ref, v_ref)


def _mla_prep(x, mod, g, wcat, qg, kvg, wuq, wukv, qng, kng, rope, *, row_base, rows_per_batch,
              emit_latent):
    n, d = x.shape
    seq = n if rows_per_batch is None else rows_per_batch
    tm = min(512, seq)
    tpb = None if rows_per_batch is None else rows_per_batch // tm
    kw = N_HEADS * HEAD_SLOT
    vw = N_HEADS * V_DIM
    q_rank, kv_rank = qg.shape[0], kvg.shape[0]
    full = lambda a: pl.BlockSpec(a.shape, lambda i: (0,) * a.ndim)
    args = [x, mod, g.reshape(1, d), wcat, qg.reshape(1, -1), kvg.reshape(1, -1), wuq, wukv, qng, kng]
    in_specs = [pl.BlockSpec((tm, d), lambda i: (i, 0))] + [full(a) for a in args[1:]]
    if rope is not None:
        pt = rows_per_batch // tm
        args += list(rope)
        in_specs += [pl.BlockSpec((tm, HEAD_SLOT), lambda i: (i % pt, 0))] * 3
    row = lambda w: pl.BlockSpec((tm, w), lambda i: (i, 0))
    out_specs = [row(kw), row(kw), row(vw)]
    out_shape = [jax.ShapeDtypeStruct((n, kw), BF16), jax.ShapeDtypeStruct((n, kw), BF16),
                 jax.ShapeDtypeStruct((n, vw), BF16)]
    if emit_latent:
        out_specs += [row(kv_rank), row(HEAD_SLOT)]
        out_shape += [jax.ShapeDtypeStruct((n, kv_rank), F32), jax.ShapeDtypeStruct((n, HEAD_SLOT), F32)]
    return pl.pallas_call(
        functools.partial(_mla_prep_kernel, row_base=row_base, tpb=tpb, use_rope=rope is not None,
                          emit_latent=emit_latent),
        grid=(n // tm,),
        in_specs=in_specs, out_specs=out_specs, out_shape=out_shape,
        compiler_params=_cparams(1),
        name="mla_prep",
    )(*args)


def _mla_cache_kernel(ckv_ref, kpe_ref, wukv_ref, kng_ref, k_ref, v_ref):
    _expand_keys(ckv_ref[...].astype(BF16), kpe_ref[...], wukv_ref, kng_ref[...], None, k_ref, v_ref)


def _mla_cache(ckv, kpe_slot, wukv, kng):
    n = ckv.shape[0]
    tm = min(512, n)
    kw = N_HEADS * HEAD_SLOT
    vw = N_HEADS * V_DIM
    row = lambda w: pl.BlockSpec((tm, w), lambda i: (i, 0))
    return pl.pallas_call(
        _mla_cache_kernel,
        grid=(n // tm,),
        in_specs=[row(ckv.shape[1]), row(HEAD_SLOT), pl.BlockSpec(wukv.shape, lambda i: (0, 0)),
                  pl.BlockSpec(kng.shape, lambda i: (0, 0))],
        out_specs=[row(kw), row(vw)],
        out_shape=[jax.ShapeDtypeStruct((n, kw), BF16), jax.ShapeDtypeStruct((n, vw), BF16)],
        compiler_params=_cparams(1),
        name="mla_cache_keys",
    )(ckv, kpe_slot, wukv, kng)


def _attn_kernel(*refs, has_cache):
    if has_cache:
        q_ref, k_ref, v_ref, kc_ref, vc_ref, o_ref = refs
    else:
        q_ref, k_ref, v_ref, o_ref = refs
    nt = (((1,), (1,)), ((), ()))
    outs = []
    for hh in range(2):
        sl = slice(hh * HEAD_SLOT, (hh + 1) * HEAD_SLOT)
        q = q_ref[:, sl]
        s = lax.dot_general(q, k_ref[:, sl], nt, preferred_element_type=F32)
        mx = jnp.max(s, axis=-1, keepdims=True)
        if has_cache:
            sc = lax.dot_general(q, kc_ref[:, sl], nt, preferred_element_type=F32)
            mx = jnp.maximum(mx, jnp.max(sc, axis=-1, keepdims=True))
        p = jnp.exp2(s - mx)
        l = jnp.sum(p, axis=-1, keepdims=True)
        o = _dot(p.astype(BF16), v_ref[...])
        if has_cache:
            pc = jnp.exp2(sc - mx)
            l = l + jnp.sum(pc, axis=-1, keepdims=True)
            o = o + _dot(pc.astype(BF16), vc_ref[...])
        outs.append(o / l)
    lane = lax.broadcasted_iota(jnp.int32, outs[0].shape, 1)
    o_ref[...] = jnp.where(lane < V_DIM, outs[0], outs[1]).astype(o_ref.dtype)


def _attention(q, k, v, kc, vc, *, batch, n, past):
    tq = min(256, n)
    nq = n // tq
    pairs = N_HEADS // 2
    has_cache = kc is not None
    in_specs = [pl.BlockSpec((tq, 2 * HEAD_SLOT), lambda b, j, i: (b * nq + i, j)),
                pl.BlockSpec((n, 2 * HEAD_SLOT), lambda b, j, i: (b, j)),
                pl.BlockSpec((n, 2 * V_DIM), lambda b, j, i: (b, j))]
    args = [q, k, v]
    if has_cache:
        in_specs += [pl.BlockSpec((past, 2 * HEAD_SLOT), lambda b, j, i: (b, j)),
                     pl.BlockSpec((past, 2 * V_DIM), lambda b, j, i: (b, j))]
        args += [kc, vc]
    return pl.pallas_call(
        functools.partial(_attn_kernel, has_cache=has_cache),
        grid=(batch, pairs, nq),
        in_specs=in_specs,
        out_specs=pl.BlockSpec((tq, 2 * V_DIM), lambda b, j, i: (b * nq + i, j)),
        out_shape=jax.ShapeDtypeStruct((batch * n, N_HEADS * V_DIM), BF16),
        compiler_params=_cparams(3),
        name="attention",
    )(*args)


def _pad_heads(w, width):
    k = w.shape[0]
    w = w.reshape(k, N_HEADS, width)
    return jnp.pad(w, ((0, 0), (0, 0), (0, HEAD_SLOT - width))).reshape(k, N_HEADS * HEAD_SLOT)


def _slot_gain(g):
    return jnp.pad(g, (0, HEAD_SLOT - QK_DIM)).reshape(1, HEAD_SLOT)


def kernel(x_prompt, x_sample, cache_ckv, cache_kpe, c, c_ctx, norm1_g, norm2_g, mod_w, mod_b, mlp_w1, mlp_w2, sc_w_in, sc_conv_w, sc_conv_b, sc_w_out, mla_w_dq, mla_q_norm_g, mla_w_uq, mla_w_dkv, mla_kv_norm_g, mla_w_ukv, mla_qn_g, mla_kn_g, mla_w_o, hy_w_in, hy_conv_w, hy_conv_b, hy_f_w0, hy_f_b0, hy_f_w1, hy_f_b1, hy_f_w2, hy_f_b2, hy_f_w3, hy_sin_freq, hy_bias, hy_w_out):
    batch, seq, d = x_prompt.shape
    dec_batch, dec_seq, _ = x_sample.shape
    depth = mod_w.shape[0]
    n_mla = mla_w_dq.shape[0]
    past = cache_ckv.shape[2]
    kv_rank = mla_kv_norm_g.shape[1]
    assert 1 + dec_batch <= MOD_ROWS

    cond = jnp.zeros((MOD_ROWS, d), F32).at[0].set(c_ctx).at[1:1 + dec_batch].set(c)
    mods = _modulation(cond, mod_w, mod_b)

    w1 = mlp_w1.astype(BF16)
    w2 = mlp_w2.astype(BF16)
    sc_in = sc_w_in.astype(BF16)
    sc_out = sc_w_out.astype(BF16)
    hy_in = hy_w_in.astype(BF16)
    hy_out = hy_w_out.astype(BF16)
    mla_o = mla_w_o.astype(BF16)

    streams = [
        dict(x=x_prompt.reshape(batch * seq, d), row_base=0, rpb=None, seq=seq, nseq=batch, ctx=True),
        dict(x=x_sample.reshape(dec_batch * dec_seq, d), row_base=1, rpb=dec_seq, seq=dec_seq,
             nseq=dec_batch, ctx=False),
    ]
    hy_consts = {}
    new_ckv, new_kpe = [], []

    for i in range(depth):
        kind, j = i % 3, i // 3
        mod = mods[i]
        for st in streams:
            x = st["x"]
            rb, rpb, n = st["row_base"], st["rpb"], st["seq"]
            if kind == 0:
                p = _nm_matmul(x, mod, norm1_g[i], sc_in[j], row_base=rb, rows_per_batch=rpb)
                x = _sconv_out(p, sc_conv_w[j], sc_conv_b[j], sc_out[j], x, mod, seq_len=n,
                               row_base=rb, rows_per_batch=rpb)
            elif kind == 1:
                pe = mla_w_dkv[j][:, kv_rank:]
                pe_slot = jnp.pad(pe, ((0, 0), (QK_NOPE, HEAD_SLOT - QK_DIM)))
                wcat = jnp.concatenate([mla_w_dq[j], mla_w_dkv[j][:, :kv_rank], pe_slot], axis=1).astype(BF16)
                wuq = _pad_heads(mla_w_uq[j], QK_DIM).astype(BF16)
                ukv = mla_w_ukv[j].reshape(kv_rank, N_HEADS, QK_NOPE + V_DIM)
                wukv = jnp.concatenate(
                    [_pad_heads(ukv[:, :, :QK_NOPE].reshape(kv_rank, -1), QK_NOPE),
                     ukv[:, :, QK_NOPE:].reshape(kv_rank, -1)], axis=1).astype(BF16)
                qng, kng = _slot_gain(mla_qn_g[j]), _slot_gain(mla_kn_g[j])
                rope = None if st["ctx"] else _rope_tables(n)
                outs = _mla_prep(x, mod, norm1_g[i], wcat, mla_q_norm_g[j], mla_kv_norm_g[j], wuq, wukv,
                                 qng, kng, rope, row_base=rb, rows_per_batch=rpb, emit_latent=st["ctx"])
                q, k, v = outs[:3]
                if st["ctx"]:
                    new_ckv.append(outs[3].reshape(batch, seq, kv_rank))
                    new_kpe.append(outs[4][:, QK_NOPE:QK_DIM].reshape(batch, seq, ROPE_DIM))
                    kc = vc = None
                else:
                    ck = cache_ckv[:, j].reshape(dec_batch * past, kv_rank)
                    cp = jnp.pad(cache_kpe[:, j].reshape(dec_batch * past, ROPE_DIM),
                                 ((0, 0), (QK_NOPE, HEAD_SLOT - QK_DIM)))
                    kc, vc = _mla_cache(ck, cp, wukv, kng)
                o = _attention(q, k, v, kc, vc, batch=st["nseq"], n=n, past=past)
                x = _mm_res(o, mla_o[j], x, mod, row_base=rb, rows_per_batch=rpb)
            else:
                m = min(HY_BLOCK, n)
                if m not in hy_consts:
                    hy_consts[m] = _dft_constants(m)
                ma, mab, minv = hy_consts[m]
                g = _hy_filter(n, m, hy_f_w0[j], hy_f_b0[j], hy_f_w1[j], hy_f_b1[j], hy_f_w2[j],
                               hy_f_b2[j], hy_f_w3[j], hy_sin_freq[j], mab)
                p = _nm_matmul(x, mod, norm1_g[i], hy_in[j], row_base=rb, rows_per_batch=rpb)
                z = _hy_conv(p, hy_conv_w[j], hy_conv_b[j], hy_bias[j], g, ma, minv, n=n, m=m,
                             batch=st["nseq"])
                x = _mm_res(z, hy_out[j], x, mod, row_base=rb, rows_per_batch=rpb)
            x = _mlp(x, mod, norm2_g[i], w1[i], w2[i], row_base=rb, rows_per_batch=rpb)
            st["x"] = x

    y_prompt = streams[0]["x"].reshape(batch, seq, d)
    y_sample = streams[1]["x"].reshape(dec_batch, dec_seq, d)
    return (y_prompt, y_sample, jnp.stack(new_ckv, axis=1), jnp.stack(new_kpe, axis=1))
```

```python
import functools
import math

import numpy as np
import jax
import jax.numpy as jnp
from jax import lax
from jax.experimental import pallas as pl
from jax.experimental.pallas import tpu as pltpu

F32 = jnp.float32
BF16 = jnp.bfloat16

EPS = 1e-6
MOD_CHUNKS = 6
N_HEADS = 16
QK_NOPE = 64
ROPE_DIM = 32
QK_DIM = QK_NOPE + ROPE_DIM
V_DIM = 64
HEAD_SLOT = 128
GRID_W = 64
ROPE_THETA = 10000.0
HY_TARGET = 1e-2
HY_FAST = 0.3
HY_SLOW = 1.5
HY_BLOCK = 512
NYQ_ROWS = 16
MOD_ROWS = 8

V7X_VMEM_BYTES = 64 * 1024 * 1024
VMEM_LIMIT = V7X_VMEM_BYTES - 8 * 1024 * 1024

SH1, SC1, G1, SH2, SC2, G2 = range(6)


def _tile(n, pref, align=128):
    if n <= pref:
        return n
    t = pref - pref % align
    while n % t:
        t -= align
    return t


def _cparams(n_axes):
    return pltpu.CompilerParams(dimension_semantics=("arbitrary",) * n_axes,
                                vmem_limit_bytes=VMEM_LIMIT)


def _mod_chunk(mod_ref, r, k, d):
    return mod_ref[pl.ds(r, 1), k * d:(k + 1) * d]


def _mod_row(i, row_base, tiles_per_batch):
    if tiles_per_batch is None:
        return row_base
    return row_base + i // tiles_per_batch


def _modnorm(x, g, sc, sh):
    ms = jnp.mean(x * x, axis=-1, keepdims=True)
    return x * lax.rsqrt(ms + EPS) * (g * (1.0 + sc)) + sh


def _rmsnorm(x, g):
    ms = jnp.mean(x * x, axis=-1, keepdims=True)
    return x * lax.rsqrt(ms + EPS) * g


def _dot(a, b):
    return jnp.dot(a, b, preferred_element_type=F32)


def _mod_kernel(c_ref, w_ref, b_ref, o_ref):
    c = c_ref[...]
    s = (c * jax.nn.sigmoid(c)).astype(BF16)
    o_ref[0] = _dot(s, w_ref[0].astype(BF16)) + b_ref[0]


def _modulation(cond, mod_w, mod_b):
    depth, d, n6 = mod_w.shape
    tn = _tile(n6, 1536)
    return pl.pallas_call(
        _mod_kernel,
        grid=(depth, n6 // tn),
        in_specs=[pl.BlockSpec((MOD_ROWS, d), lambda l, j: (0, 0)),
                  pl.BlockSpec((1, d, tn), lambda l, j: (l, 0, j)),
                  pl.BlockSpec((1, 1, tn), lambda l, j: (l, 0, j))],
        out_specs=pl.BlockSpec((1, MOD_ROWS, tn), lambda l, j: (l, 0, j)),
        out_shape=jax.ShapeDtypeStruct((depth, MOD_ROWS, n6), F32),
        compiler_params=_cparams(2),
        name="modulation",
    )(cond, mod_w, mod_b.reshape(depth, 1, n6))


def _mlp_kernel(x_ref, mod_ref, g_ref, w1_ref, w2_ref, o_ref, h_ref, acc_ref, *, row_base, tpb):
    i = pl.program_id(0)
    f = pl.program_id(1)
    d = x_ref.shape[1]
    r = _mod_row(i, row_base, tpb)

    @pl.when(f == 0)
    def _():
        h = _modnorm(x_ref[...], g_ref[...], _mod_chunk(mod_ref, r, SC2, d), _mod_chunk(mod_ref, r, SH2, d))
        h_ref[...] = h.astype(BF16)
        acc_ref[...] = jnp.zeros_like(acc_ref)

    a = jnp.maximum(_dot(h_ref[...], w1_ref[...]), 0.0)
    acc = acc_ref[...] + _dot((a * a).astype(BF16), w2_ref[...])
    acc_ref[...] = acc
    o_ref[...] = x_ref[...] + _mod_chunk(mod_ref, r, G2, d) * acc


def _mlp(x, mod, g, w1, w2, *, row_base, rows_per_batch):
    n, d = x.shape
    dff = w1.shape[1]
    tm = min(1024, n if rows_per_batch is None else rows_per_batch)
    tf = _tile(dff, 1024)
    tpb = None if rows_per_batch is None else rows_per_batch // tm
    return pl.pallas_call(
        functools.partial(_mlp_kernel, row_base=row_base, tpb=tpb),
        grid=(n // tm, dff // tf),
        in_specs=[pl.BlockSpec((tm, d), lambda i, f: (i, 0)),
                  pl.BlockSpec(mod.shape, lambda i, f: (0, 0)),
                  pl.BlockSpec((1, d), lambda i, f: (0, 0)),
                  pl.BlockSpec((d, tf), lambda i, f: (0, f)),
                  pl.BlockSpec((tf, d), lambda i, f: (f, 0))],
        out_specs=pl.BlockSpec((tm, d), lambda i, f: (i, 0)),
        out_shape=jax.ShapeDtypeStruct((n, d), F32),
        scratch_shapes=[pltpu.VMEM((tm, d), BF16), pltpu.VMEM((tm, d), F32)],
        compiler_params=_cparams(2),
        name="mlp",
    )(x, mod, g.reshape(1, d), w1, w2)


def _nm_matmul_kernel(x_ref, mod_ref, g_ref, w_ref, o_ref, h_ref, *, row_base, tpb):
    i = pl.program_id(0)
    d = x_ref.shape[1]

    @pl.when(pl.program_id(1) == 0)
    def _():
        r = _mod_row(i, row_base, tpb)
        h = _modnorm(x_ref[...], g_ref[...], _mod_chunk(mod_ref, r, SC1, d), _mod_chunk(mod_ref, r, SH1, d))
        h_ref[...] = h.astype(BF16)

    o_ref[...] = _dot(h_ref[...], w_ref[...]).astype(o_ref.dtype)


def _nm_matmul(x, mod, g, w, *, row_base, rows_per_batch):
    n, d = x.shape
    nout = w.shape[1]
    tm = min(1024, n if rows_per_batch is None else rows_per_batch)
    tn = _tile(nout, 1024)
    tpb = None if rows_per_batch is None else rows_per_batch // tm
    return pl.pallas_call(
        functools.partial(_nm_matmul_kernel, row_base=row_base, tpb=tpb),
        grid=(n // tm, nout // tn),
        in_specs=[pl.BlockSpec((tm, d), lambda i, j: (i, 0)),
                  pl.BlockSpec(mod.shape, lambda i, j: (0, 0)),
                  pl.BlockSpec((1, d), lambda i, j: (0, 0)),
                  pl.BlockSpec((d, tn), lambda i, j: (0, j))],
        out_specs=pl.BlockSpec((tm, tn), lambda i, j: (i, j)),
        out_shape=jax.ShapeDtypeStruct((n, nout), BF16),
        scratch_shapes=[pltpu.VMEM((tm, d), BF16)],
        compiler_params=_cparams(2),
        name="norm_proj",
    )(x, mod, g.reshape(1, d), w)


def _mm_res_kernel(z_ref, w_ref, x_ref, mod_ref, o_ref, *, row_base, tpb):
    d = x_ref.shape[1]
    r = _mod_row(pl.program_id(0), row_base, tpb)
    o_ref[...] = x_ref[...] + _mod_chunk(mod_ref, r, G1, d) * _dot(z_ref[...], w_ref[...])


def _mm_res(z, w, x, mod, *, row_base, rows_per_batch):
    n, d = x.shape
    k = z.shape[1]
    tm = min(1024, n if rows_per_batch is None else rows_per_batch)
    tpb = None if rows_per_batch is None else rows_per_batch // tm
    return pl.pallas_call(
        functools.partial(_mm_res_kernel, row_base=row_base, tpb=tpb),
        grid=(n // tm,),
        in_specs=[pl.BlockSpec((tm, k), lambda i: (i, 0)),
                  pl.BlockSpec((k, d), lambda i: (0, 0)),
                  pl.BlockSpec((tm, d), lambda i: (i, 0)),
                  pl.BlockSpec(mod.shape, lambda i: (0, 0))],
        out_specs=pl.BlockSpec((tm, d), lambda i: (i, 0)),
        out_shape=jax.ShapeDtypeStruct((n, d), F32),
        compiler_params=_cparams(1),
        name="out_proj_res",
    )(z, w, x, mod)


HALO = 16


def _sconv_out_kernel(bg_ref, cg_ref, xv_ref, cgp_ref, xvp_ref, cgn_ref, xvn_ref, cw_ref, cb_ref,
                      w_ref, x_ref, mod_ref, o_ref, *, seq_len, row_base, tpb):
    i = pl.program_id(0)
    tm, d = x_ref.shape
    cx = cg_ref[...].astype(F32) * xv_ref[...].astype(F32)
    prev = (cgp_ref[...].astype(F32) * xvp_ref[...].astype(F32))[HALO - 1:HALO, :]
    nxt = (cgn_ref[...].astype(F32) * xvn_ref[...].astype(F32))[0:1, :]
    row = lax.broadcasted_iota(jnp.int32, (tm, 1), 0)
    pos = jnp.bitwise_and(i * tm + row, seq_len - 1)
    dn = jnp.where(row == 0, prev, pltpu.roll(cx, 1, 0))
    dn = jnp.where(pos == 0, 0.0, dn)
    up = jnp.where(row == tm - 1, nxt, pltpu.roll(cx, tm - 1, 0))
    up = jnp.where(pos == seq_len - 1, 0.0, up)
    cw = cw_ref[...]
    u = dn * cw[0:1, :] + cx * cw[1:2, :] + up * cw[2:3, :] + cb_ref[...]
    z = (bg_ref[...].astype(F32) * u).astype(BF16)
    r = _mod_row(i, row_base, tpb)
    o_ref[...] = x_ref[...] + _mod_chunk(mod_ref, r, G1, d) * _dot(z, w_ref[...])


def _sconv_out(p, conv_w, conv_b, w_out, x, mod, *, seq_len, row_base, rows_per_batch):
    n, d = x.shape
    assert seq_len & (seq_len - 1) == 0
    tm = min(1024, n if rows_per_batch is None else rows_per_batch)
    tpb = None if rows_per_batch is None else rows_per_batch // tm
    hb = tm // HALO
    last = n // HALO - 1
    main = lambda c: pl.BlockSpec((tm, d), lambda i: (i, c))
    prev = lambda c: pl.BlockSpec((HALO, d), lambda i: (jnp.maximum(i * hb - 1, 0), c))
    nxt = lambda c: pl.BlockSpec((HALO, d), lambda i: (jnp.minimum((i + 1) * hb, last), c))
    return pl.pallas_call(
        functools.partial(_sconv_out_kernel, seq_len=seq_len, row_base=row_base, tpb=tpb),
        grid=(n // tm,),
        in_specs=[main(0), main(1), main(2), prev(1), prev(2), nxt(1), nxt(2),
                  pl.BlockSpec((3, d), lambda i: (0, 0)),
                  pl.BlockSpec((1, d), lambda i: (0, 0)),
                  pl.BlockSpec((d, d), lambda i: (0, 0)),
                  pl.BlockSpec((tm, d), lambda i: (i, 0)),
                  pl.BlockSpec(mod.shape, lambda i: (0, 0))],
        out_specs=pl.BlockSpec((tm, d), lambda i: (i, 0)),
        out_shape=jax.ShapeDtypeStruct((n, d), F32),
        compiler_params=_cparams(1),
        name="sconv_out",
    )(p, p, p, p, p, p, p, conv_w, conv_b.reshape(1, d), w_out, x, mod)


def _dft_constants(m):
    k = np.arange(m)[:, None]
    s = np.arange(m)[None, :]
    ang = np.pi * ((k * s) % (2 * m)) / m
    rows = 2 * m + NYQ_ROWS
    alt_k = (-1.0) ** np.arange(m)[:, None]
    alt_s = (-1.0) ** np.arange(m)

    def mat(re, im, drop_first):
        out = np.zeros((rows, m))
        out[:m], out[m:2 * m], out[2 * m] = re, im, alt_s
        if drop_first:
            out[:, 0] = 0.0
        return out

    cos, sin = np.cos(ang), np.sin(ang)
    mats = np.stack([mat(cos, -sin, False), mat(alt_k * cos, -alt_k * sin, True),
                     mat(cos, sin, False), mat(alt_k * cos, alt_k * sin, True)])
    ck = np.full((1, m), 2.0)
    ck[0, 0] = 1.0
    minv = np.concatenate([ck * cos.T, -ck * sin.T], axis=1) / (2 * m)
    return jnp.asarray(mats, F32).astype(BF16), jnp.asarray(minv, F32).astype(BF16)


FWD0, FWD1, BWD0, BWD1 = range(4)


def _filter_features(n, n_emb_pad):
    bands = (33 - 1) // 2
    t01 = np.linspace(0.0, 1.0, n)[:, None]
    w = 2.0 * np.pi * np.arange(n)[:, None] / n
    fb = np.linspace(1e-4, bands - 1, bands)[None, :]
    z = np.concatenate([t01, np.cos(fb * w), -np.sin(fb * w)], axis=-1)
    out = np.zeros((n, n_emb_pad))
    out[:, :z.shape[1]] = z
    return jnp.asarray(out, F32)


def _hy_filter_kernel(z_ref, w0_ref, b0_ref, w1_ref, b1_ref, w2_ref, b2_ref, fr_ref, w3b_ref, w3f_ref,
                      dl_ref, mats_ref, g_ref, a3_ref, f_ref, *, n, m):
    c = pl.program_id(0)
    dstep = pl.program_id(1)
    nb = n // m

    @pl.when(jnp.logical_and(c == 0, dstep == 0))
    def _():
        fr = fr_ref[...]
        a = jnp.sin(fr * (_dot(z_ref[...].astype(BF16), w0_ref[...]) + b0_ref[...]))
        a = jnp.sin(fr * (_dot(a.astype(BF16), w1_ref[...]) + b1_ref[...]))
        a = jnp.sin(fr * (_dot(a.astype(BF16), w2_ref[...]) + b2_ref[...]))
        a3_ref[...] = a.astype(BF16)

    @pl.when(dstep == 0)
    def _():
        t = lax.broadcasted_iota(jnp.int32, (n, 1), 0)
        decay = jnp.exp(-(t.astype(F32) / (n - 1.0)) * dl_ref[...])
        a3 = a3_ref[...]
        hb = jnp.where(t == 0, 0.0, _dot(a3, w3b_ref[...]) * decay)
        hf = _dot(a3, w3f_ref[...]) * decay
        norm = (jnp.sum(jnp.abs(hb), axis=0, keepdims=True)
                + jnp.sum(jnp.abs(hf), axis=0, keepdims=True))
        f_ref[0:n, :] = (hb / norm).astype(BF16)
        f_ref[n:2 * n, :] = (hf / norm).astype(BF16)

    d = dstep - (nb - 1)
    ia = jnp.where(d >= 0, FWD0, BWD0)
    ra = jnp.where(d >= 0, n + d * m, -d * m)
    ib = jnp.where(d >= 1, FWD1, jnp.where(d == 0, BWD0, BWD1))
    rb = jnp.where(d >= 1, n + (d - 1) * m, jnp.where(d == 0, 0, (-d - 1) * m))
    ra = pl.multiple_of(ra, m)
    rb = pl.multiple_of(rb, m)
    g = _dot(mats_ref[ia], f_ref[pl.ds(ra, m), :]) + _dot(mats_ref[ib], f_ref[pl.ds(rb, m), :])
    g_ref[0] = g.astype(g_ref.dtype)


def _hy_filter(n, m, w0, b0, w1, b1, w2, b2, w3, freq, mats):
    d = w3.shape[1] // 2
    order = w1.shape[0]
    nb = n // m
    ct = _tile(d, 256)
    emb_pad = 128
    z = _filter_features(n, emb_pad)
    w0p = jnp.zeros((emb_pad, order), F32).at[:w0.shape[0]].set(w0).astype(BF16)
    lo, hi = math.log(HY_TARGET) / HY_SLOW, math.log(HY_TARGET) / HY_FAST
    deltas = jnp.asarray(np.abs(np.linspace(lo, hi, d))[None, :], F32)
    w3b = w3.astype(BF16)
    rows = 2 * m + NYQ_ROWS
    full = lambda shp: pl.BlockSpec(shp, lambda c, s: (0,) * len(shp))
    return pl.pallas_call(
        functools.partial(_hy_filter_kernel, n=n, m=m),
        grid=(d // ct, 2 * nb - 1),
        in_specs=[full((n, emb_pad)), full((emb_pad, order)), full((1, order)),
                  full((order, order)), full((1, order)), full((order, order)), full((1, order)),
                  full((1, order)),
                  pl.BlockSpec((order, ct), lambda c, s: (0, d // ct + c)),
                  pl.BlockSpec((order, ct), lambda c, s: (0, c)),
                  pl.BlockSpec((1, ct), lambda c, s: (0, c)),
                  full((4, rows, m))],
        out_specs=pl.BlockSpec((1, rows, ct), lambda c, s: (s, 0, c)),
        out_shape=jax.ShapeDtypeStruct((2 * nb - 1, rows, d), BF16),
        scratch_shapes=[pltpu.VMEM((n, order), BF16), pltpu.VMEM((2 * n, ct), BF16)],
        compiler_params=_cparams(2),
        name="hyena_filter",
    )(z, w0p, b0.reshape(1, order), w1.astype(BF16), b1.reshape(1, order), w2.astype(BF16),
      b2.reshape(1, order), freq.reshape(1, order), w3b, w3b, deltas, mats)


def _conv3_rows(x, w, b):
    n = x.shape[0]
    row = lax.broadcasted_iota(jnp.int32, (n, 1), 0)
    dn = jnp.where(row == 0, 0.0, pltpu.roll(x, 1, 0))
    up = jnp.where(row == n - 1, 0.0, pltpu.roll(x, n - 1, 0))
    return dn * w[0:1, :] + x * w[1:2, :] + up * w[2:3, :] + b


def _hy_conv_kernel(p0_ref, p1_ref, p2_ref, w0_ref, w1_ref, w2_ref, b0_ref, b1_ref, b2_ref, hb_ref,
                    g_ref, ma_ref, minv_ref, o_ref, x0_ref, v_ref, vf_ref, *, m, nb):
    i = pl.program_id(2)

    @pl.when(i == 0)
    def _():
        x0_ref[...] = _conv3_rows(p0_ref[...].astype(F32), w0_ref[...], b0_ref[...])
        x1 = _conv3_rows(p1_ref[...].astype(F32), w1_ref[...], b1_ref[...])
        v = _conv3_rows(p2_ref[...].astype(F32), w2_ref[...], b2_ref[...])
        v_ref[...] = (v * x1).astype(BF16)
        for j in range(nb):
            vf_ref[j] = _dot(ma_ref[...], v_ref[j * m:(j + 1) * m, :])

    ct = o_ref.shape[1]
    acc_r = jnp.zeros((m, ct), F32)
    acc_i = jnp.zeros((m, ct), F32)
    acc_n = jnp.zeros((NYQ_ROWS, ct), F32)
    for j in range(nb):
        g = g_ref[i - j + nb - 1].astype(F32)
        vr = vf_ref[j, 0:m, :]
        vi = vf_ref[j, m:2 * m, :]
        gr = g[0:m, :]
        gi = g[m:2 * m, :]
        acc_r += gr * vr - gi * vi
        acc_i += gr * vi + gi * vr
        acc_n += g[2 * m:, :] * vf_ref[j, 2 * m:, :]
    spec = jnp.concatenate([acc_r, acc_i], axis=0).astype(BF16)
    t = lax.broadcasted_iota(jnp.int32, (m, 1), 0)
    sgn = (1 - 2 * jnp.bitwise_and(t, 1)).astype(F32) * (0.5 / m)
    y = _dot(minv_ref[...], spec) + sgn * acc_n[0:1, :]
    start = pl.multiple_of(i * m, m)
    v1 = v_ref[pl.ds(start, m), :].astype(F32)
    o_ref[...] = ((y + v1 * hb_ref[...]) * x0_ref[pl.ds(start, m), :]).astype(o_ref.dtype)


def _hy_conv(p, conv_w, conv_b, hy_bias, g, ma, minv, *, n, m, batch):
    d = hy_bias.shape[0]
    nb = n // m
    ct = _tile(d, 256)
    nct = d // ct
    rows = 2 * m + NYQ_ROWS
    pcol = lambda k: pl.BlockSpec((n, ct), lambda c, b, i: (b, k * nct + c))
    wcol = lambda k: pl.BlockSpec((3, ct), lambda c, b, i: (0, k * nct + c))
    bcol = lambda k: pl.BlockSpec((1, ct), lambda c, b, i: (0, k * nct + c))
    return pl.pallas_call(
        functools.partial(_hy_conv_kernel, m=m, nb=nb),
        grid=(nct, batch, nb),
        in_specs=[pcol(0), pcol(1), pcol(2), wcol(0), wcol(1), wcol(2), bcol(0), bcol(1), bcol(2),
                  pl.BlockSpec((1, ct), lambda c, b, i: (0, c)),
                  pl.BlockSpec((2 * nb - 1, rows, ct), lambda c, b, i: (0, 0, c)),
                  pl.BlockSpec((rows, m), lambda c, b, i: (0, 0)),
                  pl.BlockSpec((m, 2 * m), lambda c, b, i: (0, 0))],
        out_specs=pl.BlockSpec((m, ct), lambda c, b, i: (b * nb + i, c)),
        out_shape=jax.ShapeDtypeStruct((batch * n, d), BF16),
        scratch_shapes=[pltpu.VMEM((n, ct), F32), pltpu.VMEM((n, ct), BF16),
                        pltpu.VMEM((nb, rows, ct), F32)],
        compiler_params=_cparams(3),
        name="hyena_conv",
    )(p, p, p, conv_w, conv_w, conv_w, conv_b.reshape(1, 3 * d), conv_b.reshape(1, 3 * d),
      conv_b.reshape(1, 3 * d), hy_bias.reshape(1, d), g, ma, minv)


def _rope_tables(n):
    axis = ROPE_DIM // 2
    half = axis // 2
    inv = ROPE_THETA ** (-np.arange(0, axis, 2) / axis)
    t = np.arange(n)
    row, col = t // GRID_W, t % GRID_W
    cos = np.ones((n, HEAD_SLOT))
    sin_a = np.zeros((n, HEAD_SLOT))
    sin_b = np.zeros((n, HEAD_SLOT))
    for k, pos in enumerate((row, col)):
        ang = pos[:, None] * inv[None, :]
        base = QK_NOPE + k * axis
        cos[:, base:base + half] = np.cos(ang)
        cos[:, base + half:base + axis] = np.cos(ang)
        sin_a[:, base:base + half] = -np.sin(ang)
        sin_b[:, base + half:base + axis] = np.sin(ang)
    return jnp.asarray(cos, F32), jnp.asarray(sin_a, F32), jnp.asarray(sin_b, F32)


def _head_norm_rope(x, g, rope):
    ss = jnp.sum(x * x, axis=-1, keepdims=True)
    x = x * lax.rsqrt(ss * (1.0 / QK_DIM) + EPS) * g
    if rope is not None:
        cos, sin_a, sin_b = rope
        half = ROPE_DIM // 4
        x = x * cos + pltpu.roll(x, HEAD_SLOT - half, 1) * sin_a + pltpu.roll(x, half, 1) * sin_b
    return x


def _expand_keys(ckv_b, kpe_slot, wukv_ref, kng, rope, k_ref, v_ref):
    kw = N_HEADS * HEAD_SLOT
    kv = _dot(ckv_b, wukv_ref[...])
    for h in range(N_HEADS):
        sl = slice(h * HEAD_SLOT, (h + 1) * HEAD_SLOT)
        k_ref[:, sl] = _head_norm_rope(kv[:, sl] + kpe_slot, kng, rope).astype(BF16)
    v_ref[...] = kv[:, kw:].T.astype(BF16)


def _mla_prep_kernel(*refs, row_base, tpb, use_rope, emit_latent):
    (x_ref, mod_ref, g_ref, wcat_ref, qg_ref, kvg_ref, wuq_ref, wukv_ref, qng_ref, kng_ref), refs = refs[:10], refs[10:]
    if use_rope:
        (cos_ref, sa_ref, sb_ref), refs = refs[:3], refs[3:]
        rope = (cos_ref[...], sa_ref[...], sb_ref[...])
    else:
        rope = None
    q_ref, k_ref, v_ref = refs[:3]
    i = pl.program_id(0)
    d = x_ref.shape[1]
    r = _mod_row(i, row_base, tpb)
    h = _modnorm(x_ref[...], g_ref[...], _mod_chunk(mod_ref, r, SC1, d), _mod_chunk(mod_ref, r, SH1, d))
    a = _dot(h.astype(BF16), wcat_ref[...])
    q_rank = qg_ref.shape[1]
    kv_rank = kvg_ref.shape[1]
    qc = _rmsnorm(a[:, :q_rank], qg_ref[...]).astype(BF16)
    ckv = _rmsnorm(a[:, q_rank:q_rank + kv_rank], kvg_ref[...])
    kpe_slot = a[:, q_rank + kv_rank:]
    if emit_latent:
        refs[3][...] = ckv
        refs[4][...] = kpe_slot
    q = _dot(qc, wuq_ref[...])
    qng = qng_ref[...] * (QK_DIM ** -0.5 * math.log2(math.e))
    for hd in range(N_HEADS):
        sl = slice(hd * HEAD_SLOT, (hd + 1) * HEAD_SLOT)
        q_ref[:, sl] = _head_norm_rope(q[:, sl], qng, rope).astype(BF16)
    _expand_keys(ckv.astype(BF16), kpe_slot, wukv_ref, kng_ref[...], rope, k_ref, v_ref)


def _mla_prep(x, mod, g, wcat, qg, kvg, wuq, wukv, qng, kng, rope, *, row_base, rows_per_batch,
              emit_latent):
    n, d = x.shape
    seq = n if rows_per_batch is None else rows_per_batch
    tm = min(512, seq)
    tpb = None if rows_per_batch is None else rows_per_batch // tm
    kw = N_HEADS * HEAD_SLOT
    vw = N_HEADS * V_DIM
    q_rank, kv_rank = qg.shape[0], kvg.shape[0]
    full = lambda a: pl.BlockSpec(a.shape, lambda i: (0,) * a.ndim)
    args = [x, mod, g.reshape(1, d), wcat, qg.reshape(1, -1), kvg.reshape(1, -1), wuq, wukv, qng, kng]
    in_specs = [pl.BlockSpec((tm, d), lambda i: (i, 0))] + [full(a) for a in args[1:]]
    if rope is not None:
        pt = rows_per_batch // tm
        args += list(rope)
        in_specs += [pl.BlockSpec((tm, HEAD_SLOT), lambda i: (i % pt, 0))] * 3
    row = lambda w: pl.BlockSpec((tm, w), lambda i: (i, 0))
    out_specs = [row(kw), row(kw), pl.BlockSpec((vw, tm), lambda i: (0, i))]
    out_shape = [jax.ShapeDtypeStruct((n, kw), BF16), jax.ShapeDtypeStruct((n, kw), BF16),
                 jax.ShapeDtypeStruct((vw, n), BF16)]
    if emit_latent:
        out_specs += [row(kv_rank), row(HEAD_SLOT)]
        out_shape += [jax.ShapeDtypeStruct((n, kv_rank), F32), jax.ShapeDtypeStruct((n, HEAD_SLOT), F32)]
    return pl.pallas_call(
        functools.partial(_mla_prep_kernel, row_base=row_base, tpb=tpb, use_rope=rope is not None,
                          emit_latent=emit_latent),
        grid=(n // tm,),
        in_specs=in_specs, out_specs=out_specs, out_shape=out_shape,
        compiler_params=_cparams(1),
        name="mla_prep",
    )(*args)


def _mla_cache_kernel(ckv_ref, kpe_ref, wukv_ref, kng_ref, k_ref, v_ref):
    _expand_keys(ckv_ref[...].astype(BF16), kpe_ref[...], wukv_ref, kng_ref[...], None, k_ref, v_ref)


def _mla_cache(ckv, kpe_slot, wukv, kng):
    n = ckv.shape[0]
    tm = min(512, n)
    kw = N_HEADS * HEAD_SLOT
    vw = N_HEADS * V_DIM
    row = lambda w: pl.BlockSpec((tm, w), lambda i: (i, 0))
    return pl.pallas_call(
        _mla_cache_kernel,
        grid=(n // tm,),
        in_specs=[row(ckv.shape[1]), row(HEAD_SLOT), pl.BlockSpec(wukv.shape, lambda i: (0, 0)),
                  pl.BlockSpec(kng.shape, lambda i: (0, 0))],
        out_specs=[row(kw), pl.BlockSpec((vw, tm), lambda i: (0, i))],
        out_shape=[jax.ShapeDtypeStruct((n, kw), BF16), jax.ShapeDtypeStruct((vw, n), BF16)],
        compiler_params=_cparams(1),
        name="mla_cache_keys",
    )(ckv, kpe_slot, wukv, kng)


KEY_CHUNK = 256


def _attn_kernel(*refs, has_cache, heads):
    if has_cache:
        q_ref, k_ref, v_ref, kc_ref, vc_ref, o_ref = refs
    else:
        q_ref, k_ref, v_ref, o_ref = refs
    nt = (((1,), (1,)), ((), ()))
    n = k_ref.shape[0]
    outs = []
    for hh in range(heads):
        sl = slice(hh * HEAD_SLOT, (hh + 1) * HEAD_SLOT)
        vs = slice(hh * V_DIM, (hh + 1) * V_DIM)
        q = q_ref[:, sl]
        pieces = [(k_ref, v_ref, c, min(KEY_CHUNK, n - c)) for c in range(0, n, KEY_CHUNK)]
        if has_cache:
            pieces.append((kc_ref, vc_ref, 0, kc_ref.shape[0]))
        sts = [lax.dot_general(kr[c:c + w, sl], q, nt, preferred_element_type=F32) for kr, _, c, w in pieces]
        mx = functools.reduce(jnp.maximum, [jnp.max(st, axis=0, keepdims=True) for st in sts])
        pts = [jnp.exp2(st - mx) for st in sts]
        l = functools.reduce(jnp.add, [jnp.sum(pt, axis=0, keepdims=True) for pt in pts])
        ot = functools.reduce(jnp.add, [_dot(vr[vs, c:c + w], pt.astype(BF16))
                                        for (_, vr, c, w), pt in zip(pieces, pts)])
        outs.append(ot / l)
    o_ref[...] = jnp.concatenate(outs, axis=0).T.astype(o_ref.dtype)


def _attention(q, k, v, kc, vc, *, batch, n, past):
    has_cache = kc is not None
    tq = min(512, n)
    heads = 2 if n > tq else N_HEADS
    nq = n // tq
    groups = N_HEADS // heads
    in_specs = [pl.BlockSpec((tq, heads * HEAD_SLOT), lambda b, j, i: (b * nq + i, j)),
                pl.BlockSpec((n, heads * HEAD_SLOT), lambda b, j, i: (b, j)),
                pl.BlockSpec((heads * V_DIM, n), lambda b, j, i: (j, b))]
    args = [q, k, v]
    if has_cache:
        in_specs += [pl.BlockSpec((past, heads * HEAD_SLOT), lambda b, j, i: (b, j)),
                     pl.BlockSpec((heads * V_DIM, past), lambda b, j, i: (j, b))]
        args += [kc, vc]
    return pl.pallas_call(
        functools.partial(_attn_kernel, has_cache=has_cache, heads=heads),
        grid=(batch, groups, nq),
        in_specs=in_specs,
        out_specs=pl.BlockSpec((tq, heads * V_DIM), lambda b, j, i: (b * nq + i, j)),
        out_shape=jax.ShapeDtypeStruct((batch * n, N_HEADS * V_DIM), BF16),
        compiler_params=_cparams(3),
        name="attention",
    )(*args)


def _pad_heads(w, width):
    k = w.shape[0]
    w = w.reshape(k, N_HEADS, width)
    return jnp.pad(w, ((0, 0), (0, 0), (0, HEAD_SLOT - width))).reshape(k, N_HEADS * HEAD_SLOT)


def _slot_gain(g):
    return jnp.pad(g, (0, HEAD_SLOT - QK_DIM)).reshape(1, HEAD_SLOT)


def kernel(x_prompt, x_sample, cache_ckv, cache_kpe, c, c_ctx, norm1_g, norm2_g, mod_w, mod_b, mlp_w1, mlp_w2, sc_w_in, sc_conv_w, sc_conv_b, sc_w_out, mla_w_dq, mla_q_norm_g, mla_w_uq, mla_w_dkv, mla_kv_norm_g, mla_w_ukv, mla_qn_g, mla_kn_g, mla_w_o, hy_w_in, hy_conv_w, hy_conv_b, hy_f_w0, hy_f_b0, hy_f_w1, hy_f_b1, hy_f_w2, hy_f_b2, hy_f_w3, hy_sin_freq, hy_bias, hy_w_out):
    batch, seq, d = x_prompt.shape
    dec_batch, dec_seq, _ = x_sample.shape
    depth = mod_w.shape[0]
    n_mla = mla_w_dq.shape[0]
    past = cache_ckv.shape[2]
    kv_rank = mla_kv_norm_g.shape[1]
    assert 1 + dec_batch <= MOD_ROWS

    cond = jnp.zeros((MOD_ROWS, d), F32).at[0].set(c_ctx).at[1:1 + dec_batch].set(c)
    mods = _modulation(cond, mod_w, mod_b)

    w1 = mlp_w1.astype(BF16)
    w2 = mlp_w2.astype(BF16)
    sc_in = sc_w_in.astype(BF16)
    sc_out = sc_w_out.astype(BF16)
    hy_in = hy_w_in.astype(BF16)
    hy_out = hy_w_out.astype(BF16)
    mla_o = mla_w_o.astype(BF16)

    streams = [
        dict(x=x_prompt.reshape(batch * seq, d), row_base=0, rpb=None, seq=seq, nseq=batch, ctx=True),
        dict(x=x_sample.reshape(dec_batch * dec_seq, d), row_base=1, rpb=dec_seq, seq=dec_seq,
             nseq=dec_batch, ctx=False),
    ]
    hy_consts = {}
    new_ckv, new_kpe = [], []

    for i in range(depth):
        kind, j = i % 3, i // 3
        mod = mods[i]
        for st in streams:
            x = st["x"]
            rb, rpb, n = st["row_base"], st["rpb"], st["seq"]
            if kind == 0:
                p = _nm_matmul(x, mod, norm1_g[i], sc_in[j], row_base=rb, rows_per_batch=rpb)
                x = _sconv_out(p, sc_conv_w[j], sc_conv_b[j], sc_out[j], x, mod, seq_len=n,
                               row_base=rb, rows_per_batch=rpb)
            elif kind == 1:
                pe = mla_w_dkv[j][:, kv_rank:]
                pe_slot = jnp.pad(pe, ((0, 0), (QK_NOPE, HEAD_SLOT - QK_DIM)))
                wcat = jnp.concatenate([mla_w_dq[j], mla_w_dkv[j][:, :kv_rank], pe_slot], axis=1).astype(BF16)
                wuq = _pad_heads(mla_w_uq[j], QK_DIM).astype(BF16)
                ukv = mla_w_ukv[j].reshape(kv_rank, N_HEADS, QK_NOPE + V_DIM)
                wukv = jnp.concatenate(
                    [_pad_heads(ukv[:, :, :QK_NOPE].reshape(kv_rank, -1), QK_NOPE),
                     ukv[:, :, QK_NOPE:].reshape(kv_rank, -1)], axis=1).astype(BF16)
                qng, kng = _slot_gain(mla_qn_g[j]), _slot_gain(mla_kn_g[j])
                rope = None if st["ctx"] else _rope_tables(n)
                outs = _mla_prep(x, mod, norm1_g[i], wcat, mla_q_norm_g[j], mla_kv_norm_g[j], wuq, wukv,
                                 qng, kng, rope, row_base=rb, rows_per_batch=rpb, emit_latent=st["ctx"])
                q, k, v = outs[:3]
                if st["ctx"]:
                    new_ckv.append(outs[3].reshape(batch, seq, kv_rank))
                    new_kpe.append(outs[4][:, QK_NOPE:QK_DIM].reshape(batch, seq, ROPE_DIM))
                    kc = vc = None
                else:
                    ck = cache_ckv[:, j].reshape(dec_batch * past, kv_rank)
                    cp = jnp.pad(cache_kpe[:, j].reshape(dec_batch * past, ROPE_DIM),
                                 ((0, 0), (QK_NOPE, HEAD_SLOT - QK_DIM)))
                    kc, vc = _mla_cache(ck, cp, wukv, kng)
                o = _attention(q, k, v, kc, vc, batch=st["nseq"], n=n, past=past)
                x = _mm_res(o, mla_o[j], x, mod, row_base=rb, rows_per_batch=rpb)
            else:
                m = min(HY_BLOCK, n)
                if m not in hy_consts:
                    hy_consts[m] = _dft_constants(m)
                mats, minv = hy_consts[m]
                g = _hy_filter(n, m, hy_f_w0[j], hy_f_b0[j], hy_f_w1[j], hy_f_b1[j], hy_f_w2[j],
                               hy_f_b2[j], hy_f_w3[j], hy_sin_freq[j], mats)
                p = _nm_matmul(x, mod, norm1_g[i], hy_in[j], row_base=rb, rows_per_batch=rpb)
                z = _hy_conv(p, hy_conv_w[j], hy_conv_b[j], hy_bias[j], g, mats[FWD0], minv, n=n, m=m,
                             batch=st["nseq"])
                x = _mm_res(z, hy_out[j], x, mod, row_base=rb, rows_per_batch=rpb)
            x = _mlp(x, mod, norm2_g[i], w1[i], w2[i], row_base=rb, rows_per_batch=rpb)
            st["x"] = x

    y_prompt = streams[0]["x"].reshape(batch, seq, d)
    y_sample = streams[1]["x"].reshape(dec_batch, dec_seq, d)
    return (y_prompt, y_sample, jnp.stack(new_ckv, axis=1), jnp.stack(new_kpe, axis=1))
```

```python
import functools
import math

import numpy as np
import jax
import jax.numpy as jnp
from jax import lax
from jax.experimental import pallas as pl
from jax.experimental.pallas import tpu as pltpu

F32 = jnp.float32
BF16 = jnp.bfloat16

EPS = 1e-6
MOD_CHUNKS = 6
N_HEADS = 16
QK_NOPE = 64
ROPE_DIM = 32
QK_DIM = QK_NOPE + ROPE_DIM
V_DIM = 64
HEAD_SLOT = 128
GRID_W = 64
ROPE_THETA = 10000.0
HY_TARGET = 1e-2
HY_FAST = 0.3
HY_SLOW = 1.5
HY_BLOCK = 1024
NYQ_ROWS = 16
MOD_ROWS = 8

V7X_VMEM_BYTES = 64 * 1024 * 1024
VMEM_LIMIT = V7X_VMEM_BYTES - 8 * 1024 * 1024

SH1, SC1, G1, SH2, SC2, G2 = range(6)


def _tile(n, pref, align=128):
    if n <= pref:
        return n
    t = pref - pref % align
    while n % t:
        t -= align
    return t


def _cparams(n_axes):
    return pltpu.CompilerParams(dimension_semantics=("arbitrary",) * n_axes,
                                vmem_limit_bytes=VMEM_LIMIT)


def _mod_chunk(mod_ref, r, k, d):
    return mod_ref[pl.ds(r, 1), k * d:(k + 1) * d]


def _mod_row(i, row_base, tiles_per_batch):
    if tiles_per_batch is None:
        return row_base
    return row_base + i // tiles_per_batch


def _modnorm(x, g, sc, sh):
    ms = jnp.mean(x * x, axis=-1, keepdims=True)
    return x * lax.rsqrt(ms + EPS) * (g * (1.0 + sc)) + sh


def _rmsnorm(x, g):
    ms = jnp.mean(x * x, axis=-1, keepdims=True)
    return x * lax.rsqrt(ms + EPS) * g


def _dot(a, b):
    return jnp.dot(a, b, preferred_element_type=F32)


def _mod_kernel(c_ref, w_ref, b_ref, o_ref):
    c = c_ref[...]
    s = (c * jax.nn.sigmoid(c)).astype(BF16)
    o_ref[0] = _dot(s, w_ref[0].astype(BF16)) + b_ref[0]


def _modulation(cond, mod_w, mod_b):
    depth, d, n6 = mod_w.shape
    tn = _tile(n6, 1536)
    return pl.pallas_call(
        _mod_kernel,
        grid=(depth, n6 // tn),
        in_specs=[pl.BlockSpec((MOD_ROWS, d), lambda l, j: (0, 0)),
                  pl.BlockSpec((1, d, tn), lambda l, j: (l, 0, j)),
                  pl.BlockSpec((1, 1, tn), lambda l, j: (l, 0, j))],
        out_specs=pl.BlockSpec((1, MOD_ROWS, tn), lambda l, j: (l, 0, j)),
        out_shape=jax.ShapeDtypeStruct((depth, MOD_ROWS, n6), F32),
        compiler_params=_cparams(2),
        name="modulation",
    )(cond, mod_w, mod_b.reshape(depth, 1, n6))


def _resident(shape, index_map):
    return pl.BlockSpec(shape, index_map, pipeline_mode=pl.Buffered(1))


def _mlp_kernel(x_ref, mod_ref, g_ref, w1_ref, w2_ref, o_ref, *, row_base, tpb, tf):
    d = x_ref.shape[1]
    r = _mod_row(pl.program_id(0), row_base, tpb)
    x = x_ref[...]
    h = _modnorm(x, g_ref[...], _mod_chunk(mod_ref, r, SC2, d), _mod_chunk(mod_ref, r, SH2, d)).astype(BF16)
    acc = None
    for c in range(0, w1_ref.shape[1], tf):
        a = jnp.maximum(_dot(h, w1_ref[:, c:c + tf]), 0.0)
        y = _dot((a * a).astype(BF16), w2_ref[c:c + tf, :])
        acc = y if acc is None else acc + y
    o_ref[...] = x + _mod_chunk(mod_ref, r, G2, d) * acc


def _mlp(x, mod, g, w1, w2, layer, *, row_base, rows_per_batch):
    n, d = x.shape
    dff = w1.shape[2]
    tm = min(1024, n if rows_per_batch is None else rows_per_batch)
    tpb = None if rows_per_batch is None else rows_per_batch // tm
    return pl.pallas_call(
        functools.partial(_mlp_kernel, row_base=row_base, tpb=tpb, tf=_tile(dff, 1024)),
        grid=(n // tm,),
        in_specs=[pl.BlockSpec((tm, d), lambda i: (i, 0)),
                  pl.BlockSpec(mod.shape, lambda i: (0, 0)),
                  pl.BlockSpec((1, d), lambda i: (0, 0)),
                  _resident((None, d, dff), lambda i: (layer, 0, 0)),
                  _resident((None, dff, d), lambda i: (layer, 0, 0))],
        out_specs=pl.BlockSpec((tm, d), lambda i: (i, 0)),
        out_shape=jax.ShapeDtypeStruct((n, d), F32),
        compiler_params=_cparams(1),
        name="mlp",
    )(x, mod, g.reshape(1, d), w1, w2)


def _nm_matmul_kernel(x_ref, mod_ref, g_ref, w_ref, o_ref, *, row_base, tpb, tn):
    d = x_ref.shape[1]
    r = _mod_row(pl.program_id(0), row_base, tpb)
    h = _modnorm(x_ref[...], g_ref[...], _mod_chunk(mod_ref, r, SC1, d), _mod_chunk(mod_ref, r, SH1, d))
    h = h.astype(BF16)
    for c in range(0, o_ref.shape[1], tn):
        o_ref[:, c:c + tn] = _dot(h, w_ref[:, c:c + tn]).astype(o_ref.dtype)


def _nm_matmul(x, mod, g, w, layer, *, row_base, rows_per_batch):
    n, d = x.shape
    nout = w.shape[2]
    tm = min(1024, n if rows_per_batch is None else rows_per_batch)
    tpb = None if rows_per_batch is None else rows_per_batch // tm
    return pl.pallas_call(
        functools.partial(_nm_matmul_kernel, row_base=row_base, tpb=tpb, tn=_tile(nout, 1024)),
        grid=(n // tm,),
        in_specs=[pl.BlockSpec((tm, d), lambda i: (i, 0)),
                  pl.BlockSpec(mod.shape, lambda i: (0, 0)),
                  pl.BlockSpec((1, d), lambda i: (0, 0)),
                  _resident((None, d, nout), lambda i: (layer, 0, 0))],
        out_specs=pl.BlockSpec((tm, nout), lambda i: (i, 0)),
        out_shape=jax.ShapeDtypeStruct((n, nout), BF16),
        compiler_params=_cparams(1),
        name="norm_proj",
    )(x, mod, g.reshape(1, d), w)


def _mm_res_kernel(z_ref, w_ref, x_ref, mod_ref, o_ref, *, row_base, tpb):
    d = x_ref.shape[1]
    r = _mod_row(pl.program_id(0), row_base, tpb)
    o_ref[...] = x_ref[...] + _mod_chunk(mod_ref, r, G1, d) * _dot(z_ref[...], w_ref[...])


def _mm_res(z, w, layer, x, mod, *, row_base, rows_per_batch):
    n, d = x.shape
    k = z.shape[1]
    tm = min(1024, n if rows_per_batch is None else rows_per_batch)
    tpb = None if rows_per_batch is None else rows_per_batch // tm
    return pl.pallas_call(
        functools.partial(_mm_res_kernel, row_base=row_base, tpb=tpb),
        grid=(n // tm,),
        in_specs=[pl.BlockSpec((tm, k), lambda i: (i, 0)),
                  _resident((None, k, d), lambda i: (layer, 0, 0)),
                  pl.BlockSpec((tm, d), lambda i: (i, 0)),
                  pl.BlockSpec(mod.shape, lambda i: (0, 0))],
        out_specs=pl.BlockSpec((tm, d), lambda i: (i, 0)),
        out_shape=jax.ShapeDtypeStruct((n, d), F32),
        compiler_params=_cparams(1),
        name="out_proj_res",
    )(z, w, x, mod)


HALO = 16


def _sconv_out_kernel(bg_ref, cg_ref, xv_ref, cgp_ref, xvp_ref, cgn_ref, xvn_ref, cw_ref, cb_ref,
                      w_ref, x_ref, mod_ref, o_ref, *, seq_len, row_base, tpb):
    i = pl.program_id(0)
    tm, d = x_ref.shape
    cx = cg_ref[...].astype(F32) * xv_ref[...].astype(F32)
    prev = (cgp_ref[...].astype(F32) * xvp_ref[...].astype(F32))[HALO - 1:HALO, :]
    nxt = (cgn_ref[...].astype(F32) * xvn_ref[...].astype(F32))[0:1, :]
    row = lax.broadcasted_iota(jnp.int32, (tm, 1), 0)
    pos = jnp.bitwise_and(i * tm + row, seq_len - 1)
    dn = jnp.where(row == 0, prev, pltpu.roll(cx, 1, 0))
    dn = jnp.where(pos == 0, 0.0, dn)
    up = jnp.where(row == tm - 1, nxt, pltpu.roll(cx, tm - 1, 0))
    up = jnp.where(pos == seq_len - 1, 0.0, up)
    cw = cw_ref[...]
    u = dn * cw[0:1, :] + cx * cw[1:2, :] + up * cw[2:3, :] + cb_ref[...]
    z = (bg_ref[...].astype(F32) * u).astype(BF16)
    r = _mod_row(i, row_base, tpb)
    o_ref[...] = x_ref[...] + _mod_chunk(mod_ref, r, G1, d) * _dot(z, w_ref[...])


def _sconv_out(p, conv_w, conv_b, w_out, layer, x, mod, *, seq_len, row_base, rows_per_batch):
    n, d = x.shape
    assert seq_len & (seq_len - 1) == 0
    tm = min(1024, n if rows_per_batch is None else rows_per_batch)
    tpb = None if rows_per_batch is None else rows_per_batch // tm
    hb = tm // HALO
    last = n // HALO - 1
    main = lambda c: pl.BlockSpec((tm, d), lambda i: (i, c))
    prev = lambda c: pl.BlockSpec((HALO, d), lambda i: (jnp.maximum(i * hb - 1, 0), c))
    nxt = lambda c: pl.BlockSpec((HALO, d), lambda i: (jnp.minimum((i + 1) * hb, last), c))
    return pl.pallas_call(
        functools.partial(_sconv_out_kernel, seq_len=seq_len, row_base=row_base, tpb=tpb),
        grid=(n // tm,),
        in_specs=[main(0), main(1), main(2), prev(1), prev(2), nxt(1), nxt(2),
                  pl.BlockSpec((3, d), lambda i: (0, 0)),
                  pl.BlockSpec((1, d), lambda i: (0, 0)),
                  _resident((None, d, d), lambda i: (layer, 0, 0)),
                  pl.BlockSpec((tm, d), lambda i: (i, 0)),
                  pl.BlockSpec(mod.shape, lambda i: (0, 0))],
        out_specs=pl.BlockSpec((tm, d), lambda i: (i, 0)),
        out_shape=jax.ShapeDtypeStruct((n, d), F32),
        compiler_params=_cparams(1),
        name="sconv_out",
    )(p, p, p, p, p, p, p, conv_w, conv_b.reshape(1, d), w_out, x, mod)


def _dft_constants(m):
    k = np.arange(m)[:, None]
    s = np.arange(m)[None, :]
    ang = np.pi * ((k * s) % (2 * m)) / m
    rows = 2 * m + NYQ_ROWS
    alt_k = (-1.0) ** np.arange(m)[:, None]
    alt_s = (-1.0) ** np.arange(m)

    def mat(re, im, drop_first):
        out = np.zeros((rows, m))
        out[:m], out[m:2 * m], out[2 * m] = re, im, alt_s
        if drop_first:
            out[:, 0] = 0.0
        return out

    cos, sin = np.cos(ang), np.sin(ang)
    mats = np.stack([mat(cos, -sin, False), mat(alt_k * cos, -alt_k * sin, True),
                     mat(cos, sin, False), mat(alt_k * cos, alt_k * sin, True)])
    ck = np.full((1, m), 2.0)
    ck[0, 0] = 1.0
    minv = np.concatenate([ck * cos.T, -ck * sin.T], axis=1) / (2 * m)
    return jnp.asarray(mats, F32).astype(BF16), jnp.asarray(minv, F32).astype(BF16)


FWD0, FWD1, BWD0, BWD1 = range(4)


def _filter_features(n, n_emb_pad):
    bands = (33 - 1) // 2
    t01 = np.linspace(0.0, 1.0, n)[:, None]
    w = 2.0 * np.pi * np.arange(n)[:, None] / n
    fb = np.linspace(1e-4, bands - 1, bands)[None, :]
    z = np.concatenate([t01, np.cos(fb * w), -np.sin(fb * w)], axis=-1)
    out = np.zeros((n, n_emb_pad))
    out[:, :z.shape[1]] = z
    return jnp.asarray(out, F32)


def _hy_filter_kernel(z_ref, w0_ref, b0_ref, w1_ref, b1_ref, w2_ref, b2_ref, fr_ref, w3b_ref, w3f_ref,
                      dl_ref, mats_ref, g_ref, a3_ref, f_ref, *, n, m):
    c = pl.program_id(0)
    dstep = pl.program_id(1)
    nb = n // m

    @pl.when(jnp.logical_and(c == 0, dstep == 0))
    def _():
        fr = fr_ref[...]
        a = jnp.sin(fr * (_dot(z_ref[...].astype(BF16), w0_ref[...]) + b0_ref[...]))
        a = jnp.sin(fr * (_dot(a.astype(BF16), w1_ref[...]) + b1_ref[...]))
        a = jnp.sin(fr * (_dot(a.astype(BF16), w2_ref[...]) + b2_ref[...]))
        a3_ref[...] = a.astype(BF16)

    @pl.when(dstep == 0)
    def _():
        t = lax.broadcasted_iota(jnp.int32, (n, 1), 0)
        decay = jnp.exp(-(t.astype(F32) / (n - 1.0)) * dl_ref[...])
        a3 = a3_ref[...]
        hb = jnp.where(t == 0, 0.0, _dot(a3, w3b_ref[...]) * decay)
        hf = _dot(a3, w3f_ref[...]) * decay
        norm = (jnp.sum(jnp.abs(hb), axis=0, keepdims=True)
                + jnp.sum(jnp.abs(hf), axis=0, keepdims=True))
        f_ref[0:n, :] = (hb / norm).astype(BF16)
        f_ref[n:2 * n, :] = (hf / norm).astype(BF16)

    d = dstep - (nb - 1)
    ia = jnp.where(d >= 0, FWD0, BWD0)
    ra = jnp.where(d >= 0, n + d * m, -d * m)
    ib = jnp.where(d >= 1, FWD1, jnp.where(d == 0, BWD0, BWD1))
    rb = jnp.where(d >= 1, n + (d - 1) * m, jnp.where(d == 0, 0, (-d - 1) * m))
    ra = pl.multiple_of(ra, m)
    rb = pl.multiple_of(rb, m)
    g = _dot(mats_ref[ia], f_ref[pl.ds(ra, m), :]) + _dot(mats_ref[ib], f_ref[pl.ds(rb, m), :])
    g_ref[0] = g.astype(g_ref.dtype)


def _hy_filter(n, m, w0, b0, w1, b1, w2, b2, w3, freq, mats):
    d = w3.shape[1] // 2
    order = w1.shape[0]
    nb = n // m
    ct = _tile(d, 256)
    emb_pad = 128
    z = _filter_features(n, emb_pad)
    w0p = jnp.zeros((emb_pad, order), F32).at[:w0.shape[0]].set(w0).astype(BF16)
    lo, hi = math.log(HY_TARGET) / HY_SLOW, math.log(HY_TARGET) / HY_FAST
    deltas = jnp.asarray(np.abs(np.linspace(lo, hi, d))[None, :], F32)
    w3b = w3.astype(BF16)
    rows = 2 * m + NYQ_ROWS
    full = lambda shp: pl.BlockSpec(shp, lambda c, s: (0,) * len(shp))
    return pl.pallas_call(
        functools.partial(_hy_filter_kernel, n=n, m=m),
        grid=(d // ct, 2 * nb - 1),
        in_specs=[full((n, emb_pad)), full((emb_pad, order)), full((1, order)),
                  full((order, order)), full((1, order)), full((order, order)), full((1, order)),
                  full((1, order)),
                  pl.BlockSpec((order, ct), lambda c, s: (0, d // ct + c)),
                  pl.BlockSpec((order, ct), lambda c, s: (0, c)),
                  pl.BlockSpec((1, ct), lambda c, s: (0, c)),
                  _resident((4, rows, m), lambda c, s: (0, 0, 0))],
        out_specs=pl.BlockSpec((1, rows, ct), lambda c, s: (s, 0, c)),
        out_shape=jax.ShapeDtypeStruct((2 * nb - 1, rows, d), BF16),
        scratch_shapes=[pltpu.VMEM((n, order), BF16), pltpu.VMEM((2 * n, ct), BF16)],
        compiler_params=_cparams(2),
        name="hyena_filter",
    )(z, w0p, b0.reshape(1, order), w1.astype(BF16), b1.reshape(1, order), w2.astype(BF16),
      b2.reshape(1, order), freq.reshape(1, order), w3b, w3b, deltas, mats)


def _conv3_rows(x, w, b):
    n = x.shape[0]
    row = lax.broadcasted_iota(jnp.int32, (n, 1), 0)
    dn = jnp.where(row == 0, 0.0, pltpu.roll(x, 1, 0))
    up = jnp.where(row == n - 1, 0.0, pltpu.roll(x, n - 1, 0))
    return dn * w[0:1, :] + x * w[1:2, :] + up * w[2:3, :] + b


def _hy_conv_kernel(p0_ref, p1_ref, p2_ref, w0_ref, w1_ref, w2_ref, b0_ref, b1_ref, b2_ref, hb_ref,
                    g_ref, ma_ref, minv_ref, o_ref, x0_ref, v_ref, vf_ref, *, m, nb):
    i = pl.program_id(2)

    @pl.when(i == 0)
    def _():
        x0_ref[...] = _conv3_rows(p0_ref[...].astype(F32), w0_ref[...], b0_ref[...])
        x1 = _conv3_rows(p1_ref[...].astype(F32), w1_ref[...], b1_ref[...])
        v = _conv3_rows(p2_ref[...].astype(F32), w2_ref[...], b2_ref[...])
        vb = (v * x1).astype(BF16)
        v_ref[...] = vb
        for j in range(nb):
            vf_ref[j] = _dot(ma_ref[...], vb[j * m:(j + 1) * m, :])

    ct = o_ref.shape[1]
    acc_r = jnp.zeros((m, ct), F32)
    acc_i = jnp.zeros((m, ct), F32)
    acc_n = jnp.zeros((NYQ_ROWS, ct), F32)
    for j in range(nb):
        g = g_ref[i - j + nb - 1].astype(F32)
        vr = vf_ref[j, 0:m, :]
        vi = vf_ref[j, m:2 * m, :]
        gr = g[0:m, :]
        gi = g[m:2 * m, :]
        acc_r += gr * vr - gi * vi
        acc_i += gr * vi + gi * vr
        acc_n += g[2 * m:, :] * vf_ref[j, 2 * m:, :]
    spec = jnp.concatenate([acc_r, acc_i], axis=0).astype(BF16)
    t = lax.broadcasted_iota(jnp.int32, (m, 1), 0)
    sgn = (1 - 2 * jnp.bitwise_and(t, 1)).astype(F32) * (0.5 / m)
    y = _dot(minv_ref[...], spec) + sgn * acc_n[0:1, :]
    start = pl.multiple_of(i * m, m)
    v1 = v_ref[pl.ds(start, m), :].astype(F32)
    o_ref[...] = ((y + v1 * hb_ref[...]) * x0_ref[pl.ds(start, m), :]).astype(o_ref.dtype)


def _hy_conv(p, conv_w, conv_b, hy_bias, g, ma, minv, *, n, m, batch):
    d = hy_bias.shape[0]
    nb = n // m
    ct = _tile(d, 256)
    nct = d // ct
    rows = 2 * m + NYQ_ROWS
    pcol = lambda k: pl.BlockSpec((n, ct), lambda c, b, i: (b, k * nct + c))
    wcol = lambda k: pl.BlockSpec((3, ct), lambda c, b, i: (0, k * nct + c))
    bcol = lambda k: pl.BlockSpec((1, ct), lambda c, b, i: (0, k * nct + c))
    return pl.pallas_call(
        functools.partial(_hy_conv_kernel, m=m, nb=nb),
        grid=(nct, batch, nb),
        in_specs=[pcol(0), pcol(1), pcol(2), wcol(0), wcol(1), wcol(2), bcol(0), bcol(1), bcol(2),
                  pl.BlockSpec((1, ct), lambda c, b, i: (0, c)),
                  _resident((2 * nb - 1, rows, ct), lambda c, b, i: (0, 0, c)),
                  _resident((rows, m), lambda c, b, i: (0, 0)),
                  _resident((m, 2 * m), lambda c, b, i: (0, 0))],
        out_specs=pl.BlockSpec((m, ct), lambda c, b, i: (b * nb + i, c)),
        out_shape=jax.ShapeDtypeStruct((batch * n, d), BF16),
        scratch_shapes=[pltpu.VMEM((n, ct), F32), pltpu.VMEM((n, ct), BF16),
                        pltpu.VMEM((nb, rows, ct), F32)],
        compiler_params=_cparams(3),
        name="hyena_conv",
    )(p, p, p, conv_w, conv_w, conv_w, conv_b.reshape(1, 3 * d), conv_b.reshape(1, 3 * d),
      conv_b.reshape(1, 3 * d), hy_bias.reshape(1, d), g, ma, minv)


def _rope_tables(n):
    axis = ROPE_DIM // 2
    half = axis // 2
    inv = ROPE_THETA ** (-np.arange(0, axis, 2) / axis)
    t = np.arange(n)
    row, col = t // GRID_W, t % GRID_W
    cos = np.ones((n, HEAD_SLOT))
    sin_a = np.zeros((n, HEAD_SLOT))
    sin_b = np.zeros((n, HEAD_SLOT))
    for k, pos in enumerate((row, col)):
        ang = pos[:, None] * inv[None, :]
        base = QK_NOPE + k * axis
        cos[:, base:base + half] = np.cos(ang)
        cos[:, base + half:base + axis] = np.cos(ang)
        sin_a[:, base:base + half] = -np.sin(ang)
        sin_b[:, base + half:base + axis] = np.sin(ang)
    return jnp.asarray(cos, F32), jnp.asarray(sin_a, F32), jnp.asarray(sin_b, F32)


def _head_norm_rope(x, g, rope):
    ss = jnp.sum(x * x, axis=-1, keepdims=True)
    x = x * lax.rsqrt(ss * (1.0 / QK_DIM) + EPS) * g
    if rope is not None:
        cos, sin_a, sin_b = rope
        half = ROPE_DIM // 4
        x = x * cos + pltpu.roll(x, HEAD_SLOT - half, 1) * sin_a + pltpu.roll(x, half, 1) * sin_b
    return x


def _expand_keys(ckv_b, kpe_slot, wukv_ref, kng, rope, k_ref, v_ref):
    kw = N_HEADS * HEAD_SLOT
    kv = _dot(ckv_b, wukv_ref[...])
    if rope is not None:
        cos, sin_a, sin_b = rope
        half = ROPE_DIM // 4
        pg = kpe_slot * kng
        partner = pltpu.roll(pg, HEAD_SLOT - half, 1) * sin_a + pltpu.roll(pg, half, 1) * sin_b
        gain = kng * cos
    for h in range(N_HEADS):
        sl = slice(h * HEAD_SLOT, (h + 1) * HEAD_SLOT)
        x = kv[:, sl] + kpe_slot
        scale = lax.rsqrt(jnp.sum(x * x, axis=-1, keepdims=True) * (1.0 / QK_DIM) + EPS)
        if rope is not None:
            k_ref[:, sl] = (scale * (x * gain + partner)).astype(BF16)
        else:
            k_ref[:, sl] = (x * scale * kng).astype(BF16)
    v_ref[...] = kv[:, kw:].T.astype(BF16)


def _mla_prep_kernel(*refs, row_base, tpb, use_rope, emit_latent):
    (x_ref, mod_ref, g_ref, wcat_ref, qg_ref, kvg_ref, wuq_ref, wukv_ref, qng_ref, kng_ref), refs = refs[:10], refs[10:]
    if use_rope:
        (cos_ref, sa_ref, sb_ref), refs = refs[:3], refs[3:]
        rope = (cos_ref[...], sa_ref[...], sb_ref[...])
    else:
        rope = None
    q_ref, k_ref, v_ref = refs[:3]
    i = pl.program_id(0)
    d = x_ref.shape[1]
    r = _mod_row(i, row_base, tpb)
    h = _modnorm(x_ref[...], g_ref[...], _mod_chunk(mod_ref, r, SC1, d), _mod_chunk(mod_ref, r, SH1, d))
    a = _dot(h.astype(BF16), wcat_ref[...])
    q_rank = qg_ref.shape[1]
    kv_rank = kvg_ref.shape[1]
    qc = _rmsnorm(a[:, :q_rank], qg_ref[...]).astype(BF16)
    ckv = _rmsnorm(a[:, q_rank:q_rank + kv_rank], kvg_ref[...])
    kpe_slot = a[:, q_rank + kv_rank:]
    if emit_latent:
        refs[3][...] = ckv
        refs[4][...] = kpe_slot
    q = _dot(qc, wuq_ref[...])
    qng = qng_ref[...] * (QK_DIM ** -0.5 * math.log2(math.e))
    for hd in range(N_HEADS):
        sl = slice(hd * HEAD_SLOT, (hd + 1) * HEAD_SLOT)
        q_ref[:, sl] = _head_norm_rope(q[:, sl], qng, rope).astype(BF16)
    _expand_keys(ckv.astype(BF16), kpe_slot, wukv_ref, kng_ref[...], rope, k_ref, v_ref)


def _mla_prep(x, mod, g, wcat, qg, kvg, wuq, wukv, qng, kng, rope, *, row_base, rows_per_batch,
              emit_latent):
    n, d = x.shape
    seq = n if rows_per_batch is None else rows_per_batch
    tm = min(512, seq)
    tpb = None if rows_per_batch is None else rows_per_batch // tm
    kw = N_HEADS * HEAD_SLOT
    vw = N_HEADS * V_DIM
    q_rank, kv_rank = qg.shape[0], kvg.shape[0]
    full = lambda a: pl.BlockSpec(a.shape, lambda i: (0,) * a.ndim)
    args = [x, mod, g.reshape(1, d), wcat, qg.reshape(1, -1), kvg.reshape(1, -1), wuq, wukv, qng, kng]
    in_specs = [pl.BlockSpec((tm, d), lambda i: (i, 0))] + [full(a) for a in args[1:]]
    if rope is not None:
        pt = rows_per_batch // tm
        args += list(rope)
        in_specs += [pl.BlockSpec((tm, HEAD_SLOT), lambda i: (i % pt, 0))] * 3
    row = lambda w: pl.BlockSpec((tm, w), lambda i: (i, 0))
    out_specs = [row(kw), row(kw), pl.BlockSpec((vw, tm), lambda i: (0, i))]
    out_shape = [jax.ShapeDtypeStruct((n, kw), BF16), jax.ShapeDtypeStruct((n, kw), BF16),
                 jax.ShapeDtypeStruct((vw, n), BF16)]
    if emit_latent:
        out_specs += [row(kv_rank), row(HEAD_SLOT)]
        out_shape += [jax.ShapeDtypeStruct((n, kv_rank), F32), jax.ShapeDtypeStruct((n, HEAD_SLOT), F32)]
    return pl.pallas_call(
        functools.partial(_mla_prep_kernel, row_base=row_base, tpb=tpb, use_rope=rope is not None,
                          emit_latent=emit_latent),
        grid=(n // tm,),
        in_specs=in_specs, out_specs=out_specs, out_shape=out_shape,
        compiler_params=_cparams(1),
        name="mla_prep",
    )(*args)


def _mla_cache_kernel(ckv_ref, kpe_ref, wukv_ref, kng_ref, k_ref, v_ref):
    _expand_keys(ckv_ref[...].astype(BF16), kpe_ref[...], wukv_ref, kng_ref[...], None, k_ref, v_ref)


def _mla_cache(ckv, kpe_slot, wukv, kng):
    n = ckv.shape[0]
    tm = min(512, n)
    kw = N_HEADS * HEAD_SLOT
    vw = N_HEADS * V_DIM
    row = lambda w: pl.BlockSpec((tm, w), lambda i: (i, 0))
    return pl.pallas_call(
        _mla_cache_kernel,
        grid=(n // tm,),
        in_specs=[row(ckv.shape[1]), row(HEAD_SLOT), pl.BlockSpec(wukv.shape, lambda i: (0, 0)),
                  pl.BlockSpec(kng.shape, lambda i: (0, 0))],
        out_specs=[row(kw), pl.BlockSpec((vw, tm), lambda i: (0, i))],
        out_shape=[jax.ShapeDtypeStruct((n, kw), BF16), jax.ShapeDtypeStruct((vw, n), BF16)],
        compiler_params=_cparams(1),
        name="mla_cache_keys",
    )(ckv, kpe_slot, wukv, kng)


KEY_CHUNK = 256


def _attn_kernel(*refs, has_cache, heads):
    if has_cache:
        q_ref, k_ref, v_ref, kc_ref, vc_ref, o_ref = refs
    else:
        q_ref, k_ref, v_ref, o_ref = refs
    nt = (((1,), (1,)), ((), ()))
    n = k_ref.shape[0]
    outs = []
    for hh in range(heads):
        sl = slice(hh * HEAD_SLOT, (hh + 1) * HEAD_SLOT)
        vs = slice(hh * V_DIM, (hh + 1) * V_DIM)
        q = q_ref[:, sl]
        pieces = [(k_ref, v_ref, c, min(KEY_CHUNK, n - c)) for c in range(0, n, KEY_CHUNK)]
        if has_cache:
            pieces.append((kc_ref, vc_ref, 0, kc_ref.shape[0]))
        sts = [lax.dot_general(kr[c:c + w, sl], q, nt, preferred_element_type=F32) for kr, _, c, w in pieces]
        mx = functools.reduce(jnp.maximum, [jnp.max(st, axis=0, keepdims=True) for st in sts])
        pts = [jnp.exp2(st - mx) for st in sts]
        l = functools.reduce(jnp.add, [jnp.sum(pt, axis=0, keepdims=True) for pt in pts])
        ot = functools.reduce(jnp.add, [_dot(vr[vs, c:c + w], pt.astype(BF16))
                                        for (_, vr, c, w), pt in zip(pieces, pts)])
        outs.append(ot / l)
    o_ref[...] = jnp.concatenate(outs, axis=0).T.astype(o_ref.dtype)


def _attention(q, k, v, kc, vc, *, batch, n, past):
    has_cache = kc is not None
    tq = min(512, n)
    heads = 2 if n > tq else N_HEADS
    nq = n // tq
    groups = N_HEADS // heads
    in_specs = [pl.BlockSpec((tq, heads * HEAD_SLOT), lambda b, j, i: (b * nq + i, j)),
                pl.BlockSpec((n, heads * HEAD_SLOT), lambda b, j, i: (b, j)),
                pl.BlockSpec((heads * V_DIM, n), lambda b, j, i: (j, b))]
    args = [q, k, v]
    if has_cache:
        in_specs += [pl.BlockSpec((past, heads * HEAD_SLOT), lambda b, j, i: (b, j)),
                     pl.BlockSpec((heads * V_DIM, past), lambda b, j, i: (j, b))]
        args += [kc, vc]
    return pl.pallas_call(
        functools.partial(_attn_kernel, has_cache=has_cache, heads=heads),
        grid=(batch, groups, nq),
        in_specs=in_specs,
        out_specs=pl.BlockSpec((tq, heads * V_DIM), lambda b, j, i: (b * nq + i, j)),
        out_shape=jax.ShapeDtypeStruct((batch * n, N_HEADS * V_DIM), BF16),
        compiler_params=_cparams(3),
        name="attention",
    )(*args)


def _pad_heads(w, width):
    k = w.shape[0]
    w = w.reshape(k, N_HEADS, width)
    return jnp.pad(w, ((0, 0), (0, 0), (0, HEAD_SLOT - width))).reshape(k, N_HEADS * HEAD_SLOT)


def _slot_gain(g):
    return jnp.pad(g, (0, HEAD_SLOT - QK_DIM)).reshape(1, HEAD_SLOT)


def kernel(x_prompt, x_sample, cache_ckv, cache_kpe, c, c_ctx, norm1_g, norm2_g, mod_w, mod_b, mlp_w1, mlp_w2, sc_w_in, sc_conv_w, sc_conv_b, sc_w_out, mla_w_dq, mla_q_norm_g, mla_w_uq, mla_w_dkv, mla_kv_norm_g, mla_w_ukv, mla_qn_g, mla_kn_g, mla_w_o, hy_w_in, hy_conv_w, hy_conv_b, hy_f_w0, hy_f_b0, hy_f_w1, hy_f_b1, hy_f_w2, hy_f_b2, hy_f_w3, hy_sin_freq, hy_bias, hy_w_out):
    batch, seq, d = x_prompt.shape
    dec_batch, dec_seq, _ = x_sample.shape
    depth = mod_w.shape[0]
    n_mla = mla_w_dq.shape[0]
    past = cache_ckv.shape[2]
    kv_rank = mla_kv_norm_g.shape[1]
    assert 1 + dec_batch <= MOD_ROWS

    cond = jnp.zeros((MOD_ROWS, d), F32).at[0].set(c_ctx).at[1:1 + dec_batch].set(c)
    mods = _modulation(cond, mod_w, mod_b)

    w1 = mlp_w1.astype(BF16)
    w2 = mlp_w2.astype(BF16)
    sc_in = sc_w_in.astype(BF16)
    sc_out = sc_w_out.astype(BF16)
    hy_in = hy_w_in.astype(BF16)
    hy_out = hy_w_out.astype(BF16)
    mla_o = mla_w_o.astype(BF16)

    streams = [
        dict(x=x_prompt.reshape(batch * seq, d), row_base=0, rpb=None, seq=seq, nseq=batch, ctx=True),
        dict(x=x_sample.reshape(dec_batch * dec_seq, d), row_base=1, rpb=dec_seq, seq=dec_seq,
             nseq=dec_batch, ctx=False),
    ]
    hy_consts = {}
    new_ckv, new_kpe = [], []

    for i in range(depth):
        kind, j = i % 3, i // 3
        mod = mods[i]
        for st in streams:
            x = st["x"]
            rb, rpb, n = st["row_base"], st["rpb"], st["seq"]
            if kind == 0:
                p = _nm_matmul(x, mod, norm1_g[i], sc_in, j, row_base=rb, rows_per_batch=rpb)
                x = _sconv_out(p, sc_conv_w[j], sc_conv_b[j], sc_out, j, x, mod, seq_len=n,
                               row_base=rb, rows_per_batch=rpb)
            elif kind == 1:
                pe = mla_w_dkv[j][:, kv_rank:]
                pe_slot = jnp.pad(pe, ((0, 0), (QK_NOPE, HEAD_SLOT - QK_DIM)))
                wcat = jnp.concatenate([mla_w_dq[j], mla_w_dkv[j][:, :kv_rank], pe_slot], axis=1).astype(BF16)
                wuq = _pad_heads(mla_w_uq[j], QK_DIM).astype(BF16)
                ukv = mla_w_ukv[j].reshape(kv_rank, N_HEADS, QK_NOPE + V_DIM)
                wukv = jnp.concatenate(
                    [_pad_heads(ukv[:, :, :QK_NOPE].reshape(kv_rank, -1), QK_NOPE),
                     ukv[:, :, QK_NOPE:].reshape(kv_rank, -1)], axis=1).astype(BF16)
                qng, kng = _slot_gain(mla_qn_g[j]), _slot_gain(mla_kn_g[j])
                rope = None if st["ctx"] else _rope_tables(n)
                outs = _mla_prep(x, mod, norm1_g[i], wcat, mla_q_norm_g[j], mla_kv_norm_g[j], wuq, wukv,
                                 qng, kng, rope, row_base=rb, rows_per_batch=rpb, emit_latent=st["ctx"])
                q, k, v = outs[:3]
                if st["ctx"]:
                    new_ckv.append(outs[3].reshape(batch, seq, kv_rank))
                    new_kpe.append(outs[4][:, QK_NOPE:QK_DIM].reshape(batch, seq, ROPE_DIM))
                    kc = vc = None
                else:
                    ck = cache_ckv[:, j].reshape(dec_batch * past, kv_rank)
                    cp = jnp.pad(cache_kpe[:, j].reshape(dec_batch * past, ROPE_DIM),
                                 ((0, 0), (QK_NOPE, HEAD_SLOT - QK_DIM)))
                    kc, vc = _mla_cache(ck, cp, wukv, kng)
                o = _attention(q, k, v, kc, vc, batch=st["nseq"], n=n, past=past)
                x = _mm_res(o, mla_o, j, x, mod, row_base=rb, rows_per_batch=rpb)
            else:
                m = min(HY_BLOCK, n)
                if m not in hy_consts:
                    hy_consts[m] = _dft_constants(m)
                mats, minv = hy_consts[m]
                g = _hy_filter(n, m, hy_f_w0[j], hy_f_b0[j], hy_f_w1[j], hy_f_b1[j], hy_f_w2[j],
                               hy_f_b2[j], hy_f_w3[j], hy_sin_freq[j], mats)
                p = _nm_matmul(x, mod, norm1_g[i], hy_in, j, row_base=rb, rows_per_batch=rpb)
                z = _hy_conv(p, hy_conv_w[j], hy_conv_b[j], hy_bias[j], g, mats[FWD0], minv, n=n, m=m,
                             batch=st["nseq"])
                x = _mm_res(z, hy_out, j, x, mod, row_base=rb, rows_per_batch=rpb)
            x = _mlp(x, mod, norm2_g[i], w1, w2, i, row_base=rb, rows_per_batch=rpb)
            st["x"] = x

    y_prompt = streams[0]["x"].reshape(batch, seq, d)
    y_sample = streams[1]["x"].reshape(dec_batch, dec_seq, d)
    return (y_prompt, y_sample, jnp.stack(new_ckv, axis=1), jnp.stack(new_kpe, axis=1))
```

```python
import functools
import math

import numpy as np
import jax
import jax.numpy as jnp
from jax import lax
from jax.experimental import pallas as pl
from jax.experimental.pallas import tpu as pltpu

F32 = jnp.float32
BF16 = jnp.bfloat16

EPS = 1e-6
MOD_CHUNKS = 6
N_HEADS = 16
QK_NOPE = 64
ROPE_DIM = 32
QK_DIM = QK_NOPE + ROPE_DIM
V_DIM = 64
HEAD_SLOT = 128
GRID_W = 64
ROPE_THETA = 10000.0
HY_TARGET = 1e-2
HY_FAST = 0.3
HY_SLOW = 1.5
HY_BLOCK = 1024
NYQ_ROWS = 16
MOD_ROWS = 8

V7X_VMEM_BYTES = 64 * 1024 * 1024
VMEM_LIMIT = V7X_VMEM_BYTES - 8 * 1024 * 1024

SH1, SC1, G1, SH2, SC2, G2 = range(6)


def _tile(n, pref, align=128):
    if n <= pref:
        return n
    t = pref - pref % align
    while n % t:
        t -= align
    return t


def _cparams(n_axes):
    return pltpu.CompilerParams(dimension_semantics=("arbitrary",) * n_axes,
                                vmem_limit_bytes=VMEM_LIMIT)


def _mod_chunk(mod_ref, r, k, d):
    return mod_ref[pl.ds(r, 1), k * d:(k + 1) * d]


def _mod_row(i, row_base, tiles_per_batch):
    if tiles_per_batch is None:
        return row_base
    return row_base + i // tiles_per_batch


def _modnorm(x, g, sc, sh):
    ms = jnp.mean(x * x, axis=-1, keepdims=True)
    return x * lax.rsqrt(ms + EPS) * (g * (1.0 + sc)) + sh


def _rmsnorm(x, g):
    ms = jnp.mean(x * x, axis=-1, keepdims=True)
    return x * lax.rsqrt(ms + EPS) * g


def _dot(a, b):
    return jnp.dot(a, b, preferred_element_type=F32)


def _mod_kernel(c_ref, w_ref, b_ref, o_ref):
    c = c_ref[...]
    s = (c * jax.nn.sigmoid(c)).astype(BF16)
    o_ref[0] = _dot(s, w_ref[0].astype(BF16)) + b_ref[0]


def _modulation(cond, mod_w, mod_b):
    depth, d, n6 = mod_w.shape
    tn = _tile(n6, 1536)
    return pl.pallas_call(
        _mod_kernel,
        grid=(depth, n6 // tn),
        in_specs=[pl.BlockSpec((MOD_ROWS, d), lambda l, j: (0, 0)),
                  pl.BlockSpec((1, d, tn), lambda l, j: (l, 0, j)),
                  pl.BlockSpec((1, 1, tn), lambda l, j: (l, 0, j))],
        out_specs=pl.BlockSpec((1, MOD_ROWS, tn), lambda l, j: (l, 0, j)),
        out_shape=jax.ShapeDtypeStruct((depth, MOD_ROWS, n6), F32),
        compiler_params=_cparams(2),
        name="modulation",
    )(cond, mod_w, mod_b.reshape(depth, 1, n6))


def _resident(shape, index_map):
    return pl.BlockSpec(shape, index_map, pipeline_mode=pl.Buffered(1))


def _mlp_kernel(x_ref, mod_ref, g_ref, w1_ref, w2_ref, o_ref, *, row_base, tpb, tf):
    d = x_ref.shape[1]
    r = _mod_row(pl.program_id(0), row_base, tpb)
    x = x_ref[...]
    h = _modnorm(x, g_ref[...], _mod_chunk(mod_ref, r, SC2, d), _mod_chunk(mod_ref, r, SH2, d)).astype(BF16)
    acc = None
    for c in range(0, w1_ref.shape[1], tf):
        a = jnp.maximum(_dot(h, w1_ref[:, c:c + tf]), 0.0)
        y = _dot((a * a).astype(BF16), w2_ref[c:c + tf, :])
        acc = y if acc is None else acc + y
    o_ref[...] = x + _mod_chunk(mod_ref, r, G2, d) * acc


def _mlp(x, mod, g, w1, w2, layer, *, row_base, rows_per_batch):
    n, d = x.shape
    dff = w1.shape[2]
    tm = min(1024, n if rows_per_batch is None else rows_per_batch)
    tpb = None if rows_per_batch is None else rows_per_batch // tm
    return pl.pallas_call(
        functools.partial(_mlp_kernel, row_base=row_base, tpb=tpb, tf=_tile(dff, 1024)),
        grid=(n // tm,),
        in_specs=[pl.BlockSpec((tm, d), lambda i: (i, 0)),
                  pl.BlockSpec(mod.shape, lambda i: (0, 0)),
                  pl.BlockSpec((1, d), lambda i: (0, 0)),
                  _resident((None, d, dff), lambda i: (layer, 0, 0)),
                  _resident((None, dff, d), lambda i: (layer, 0, 0))],
        out_specs=pl.BlockSpec((tm, d), lambda i: (i, 0)),
        out_shape=jax.ShapeDtypeStruct((n, d), F32),
        compiler_params=_cparams(1),
        name="mlp",
    )(x, mod, g.reshape(1, d), w1, w2)


def _nm_matmul_kernel(x_ref, mod_ref, g_ref, w_ref, o_ref, *, row_base, tpb, tn):
    d = x_ref.shape[1]
    r = _mod_row(pl.program_id(0), row_base, tpb)
    h = _modnorm(x_ref[...], g_ref[...], _mod_chunk(mod_ref, r, SC1, d), _mod_chunk(mod_ref, r, SH1, d))
    h = h.astype(BF16)
    for c in range(0, o_ref.shape[1], tn):
        o_ref[:, c:c + tn] = _dot(h, w_ref[:, c:c + tn]).astype(o_ref.dtype)


def _nm_matmul(x, mod, g, w, layer, *, row_base, rows_per_batch):
    n, d = x.shape
    nout = w.shape[2]
    tm = min(1024, n if rows_per_batch is None else rows_per_batch)
    tpb = None if rows_per_batch is None else rows_per_batch // tm
    return pl.pallas_call(
        functools.partial(_nm_matmul_kernel, row_base=row_base, tpb=tpb, tn=_tile(nout, 1024)),
        grid=(n // tm,),
        in_specs=[pl.BlockSpec((tm, d), lambda i: (i, 0)),
                  pl.BlockSpec(mod.shape, lambda i: (0, 0)),
                  pl.BlockSpec((1, d), lambda i: (0, 0)),
                  _resident((None, d, nout), lambda i: (layer, 0, 0))],
        out_specs=pl.BlockSpec((tm, nout), lambda i: (i, 0)),
        out_shape=jax.ShapeDtypeStruct((n, nout), BF16),
        compiler_params=_cparams(1),
        name="norm_proj",
    )(x, mod, g.reshape(1, d), w)


def _mm_res_kernel(z_ref, w_ref, x_ref, mod_ref, o_ref, *, row_base, tpb):
    d = x_ref.shape[1]
    r = _mod_row(pl.program_id(0), row_base, tpb)
    o_ref[...] = x_ref[...] + _mod_chunk(mod_ref, r, G1, d) * _dot(z_ref[...], w_ref[...])


def _mm_res(z, w, layer, x, mod, *, row_base, rows_per_batch):
    n, d = x.shape
    k = z.shape[1]
    tm = min(1024, n if rows_per_batch is None else rows_per_batch)
    tpb = None if rows_per_batch is None else rows_per_batch // tm
    return pl.pallas_call(
        functools.partial(_mm_res_kernel, row_base=row_base, tpb=tpb),
        grid=(n // tm,),
        in_specs=[pl.BlockSpec((tm, k), lambda i: (i, 0)),
                  _resident((None, k, d), lambda i: (layer, 0, 0)),
                  pl.BlockSpec((tm, d), lambda i: (i, 0)),
                  pl.BlockSpec(mod.shape, lambda i: (0, 0))],
        out_specs=pl.BlockSpec((tm, d), lambda i: (i, 0)),
        out_shape=jax.ShapeDtypeStruct((n, d), F32),
        compiler_params=_cparams(1),
        name="out_proj_res",
    )(z, w, x, mod)


HALO = 16


def _sconv_out_kernel(bg_ref, cg_ref, xv_ref, cgp_ref, xvp_ref, cgn_ref, xvn_ref, cw_ref, cb_ref,
                      w_ref, x_ref, mod_ref, o_ref, *, seq_len, row_base, tpb):
    i = pl.program_id(0)
    tm, d = x_ref.shape
    cx = cg_ref[...].astype(F32) * xv_ref[...].astype(F32)
    prev = (cgp_ref[...].astype(F32) * xvp_ref[...].astype(F32))[HALO - 1:HALO, :]
    nxt = (cgn_ref[...].astype(F32) * xvn_ref[...].astype(F32))[0:1, :]
    row = lax.broadcasted_iota(jnp.int32, (tm, 1), 0)
    pos = jnp.bitwise_and(i * tm + row, seq_len - 1)
    dn = jnp.where(row == 0, prev, pltpu.roll(cx, 1, 0))
    dn = jnp.where(pos == 0, 0.0, dn)
    up = jnp.where(row == tm - 1, nxt, pltpu.roll(cx, tm - 1, 0))
    up = jnp.where(pos == seq_len - 1, 0.0, up)
    cw = cw_ref[...]
    u = dn * cw[0:1, :] + cx * cw[1:2, :] + up * cw[2:3, :] + cb_ref[...]
    z = (bg_ref[...].astype(F32) * u).astype(BF16)
    r = _mod_row(i, row_base, tpb)
    o_ref[...] = x_ref[...] + _mod_chunk(mod_ref, r, G1, d) * _dot(z, w_ref[...])


def _sconv_out(p, conv_w, conv_b, w_out, layer, x, mod, *, seq_len, row_base, rows_per_batch):
    n, d = x.shape
    assert seq_len & (seq_len - 1) == 0
    tm = min(1024, n if rows_per_batch is None else rows_per_batch)
    tpb = None if rows_per_batch is None else rows_per_batch // tm
    hb = tm // HALO
    last = n // HALO - 1
    main = lambda c: pl.BlockSpec((tm, d), lambda i: (i, c))
    prev = lambda c: pl.BlockSpec((HALO, d), lambda i: (jnp.maximum(i * hb - 1, 0), c))
    nxt = lambda c: pl.BlockSpec((HALO, d), lambda i: (jnp.minimum((i + 1) * hb, last), c))
    return pl.pallas_call(
        functools.partial(_sconv_out_kernel, seq_len=seq_len, row_base=row_base, tpb=tpb),
        grid=(n // tm,),
        in_specs=[main(0), main(1), main(2), prev(1), prev(2), nxt(1), nxt(2),
                  pl.BlockSpec((3, d), lambda i: (0, 0)),
                  pl.BlockSpec((1, d), lambda i: (0, 0)),
                  _resident((None, d, d), lambda i: (layer, 0, 0)),
                  pl.BlockSpec((tm, d), lambda i: (i, 0)),
                  pl.BlockSpec(mod.shape, lambda i: (0, 0))],
        out_specs=pl.BlockSpec((tm, d), lambda i: (i, 0)),
        out_shape=jax.ShapeDtypeStruct((n, d), F32),
        compiler_params=_cparams(1),
        name="sconv_out",
    )(p, p, p, p, p, p, p, conv_w, conv_b.reshape(1, d), w_out, x, mod)


def _dft_constants(m):
    k = np.arange(m)[:, None]
    s = np.arange(m)[None, :]
    ang = np.pi * ((k * s) % (2 * m)) / m
    rows = 2 * m + NYQ_ROWS
    alt_k = (-1.0) ** np.arange(m)[:, None]
    alt_s = (-1.0) ** np.arange(m)

    def mat(re, im, drop_first):
        out = np.zeros((rows, m))
        out[:m], out[m:2 * m], out[2 * m] = re, im, alt_s
        if drop_first:
            out[:, 0] = 0.0
        return out

    cos, sin = np.cos(ang), np.sin(ang)
    mats = np.stack([mat(cos, -sin, False), mat(alt_k * cos, -alt_k * sin, True),
                     mat(cos, sin, False), mat(alt_k * cos, alt_k * sin, True)])
    ck = np.full((1, m), 2.0)
    ck[0, 0] = 1.0
    minv = np.concatenate([ck * cos.T, -ck * sin.T], axis=1) / (2 * m)
    return jnp.asarray(mats, F32).astype(BF16), jnp.asarray(minv, F32).astype(BF16)


FWD0, FWD1, BWD0, BWD1 = range(4)


def _filter_features(n, n_emb_pad):
    bands = (33 - 1) // 2
    t01 = np.linspace(0.0, 1.0, n)[:, None]
    w = 2.0 * np.pi * np.arange(n)[:, None] / n
    fb = np.linspace(1e-4, bands - 1, bands)[None, :]
    z = np.concatenate([t01, np.cos(fb * w), -np.sin(fb * w)], axis=-1)
    out = np.zeros((n, n_emb_pad))
    out[:, :z.shape[1]] = z
    return jnp.asarray(out, F32)


def _hy_filter_kernel(z_ref, w0_ref, b0_ref, w1_ref, b1_ref, w2_ref, b2_ref, fr_ref, w3b_ref, w3f_ref,
                      dl_ref, mats_ref, g_ref, a3_ref, f_ref, *, n, m):
    c = pl.program_id(0)
    dstep = pl.program_id(1)
    nb = n // m

    @pl.when(jnp.logical_and(c == 0, dstep == 0))
    def _():
        fr = fr_ref[...]
        a = jnp.sin(fr * (_dot(z_ref[...].astype(BF16), w0_ref[...]) + b0_ref[...]))
        a = jnp.sin(fr * (_dot(a.astype(BF16), w1_ref[...]) + b1_ref[...]))
        a = jnp.sin(fr * (_dot(a.astype(BF16), w2_ref[...]) + b2_ref[...]))
        a3_ref[...] = a.astype(BF16)

    @pl.when(dstep == 0)
    def _():
        t = lax.broadcasted_iota(jnp.int32, (n, 1), 0)
        decay = jnp.exp(-(t.astype(F32) / (n - 1.0)) * dl_ref[...])
        a3 = a3_ref[...]
        hb = jnp.where(t == 0, 0.0, _dot(a3, w3b_ref[...]) * decay)
        hf = _dot(a3, w3f_ref[...]) * decay
        norm = (jnp.sum(jnp.abs(hb), axis=0, keepdims=True)
                + jnp.sum(jnp.abs(hf), axis=0, keepdims=True))
        f_ref[0:n, :] = (hb / norm).astype(BF16)
        f_ref[n:2 * n, :] = (hf / norm).astype(BF16)

    d = dstep - (nb - 1)
    ia = jnp.where(d >= 0, FWD0, BWD0)
    ra = jnp.where(d >= 0, n + d * m, -d * m)
    ib = jnp.where(d >= 1, FWD1, jnp.where(d == 0, BWD0, BWD1))
    rb = jnp.where(d >= 1, n + (d - 1) * m, jnp.where(d == 0, 0, (-d - 1) * m))
    ra = pl.multiple_of(ra, m)
    rb = pl.multiple_of(rb, m)
    g = _dot(mats_ref[ia], f_ref[pl.ds(ra, m), :]) + _dot(mats_ref[ib], f_ref[pl.ds(rb, m), :])
    g_ref[0] = g.astype(g_ref.dtype)


def _hy_filter(n, m, w0, b0, w1, b1, w2, b2, w3, freq, mats):
    d = w3.shape[1] // 2
    order = w1.shape[0]
    nb = n // m
    ct = _tile(d, 256)
    emb_pad = 128
    z = _filter_features(n, emb_pad)
    w0p = jnp.zeros((emb_pad, order), F32).at[:w0.shape[0]].set(w0).astype(BF16)
    lo, hi = math.log(HY_TARGET) / HY_SLOW, math.log(HY_TARGET) / HY_FAST
    deltas = jnp.asarray(np.abs(np.linspace(lo, hi, d))[None, :], F32)
    w3b = w3.astype(BF16)
    rows = 2 * m + NYQ_ROWS
    full = lambda shp: pl.BlockSpec(shp, lambda c, s: (0,) * len(shp))
    return pl.pallas_call(
        functools.partial(_hy_filter_kernel, n=n, m=m),
        grid=(d // ct, 2 * nb - 1),
        in_specs=[full((n, emb_pad)), full((emb_pad, order)), full((1, order)),
                  full((order, order)), full((1, order)), full((order, order)), full((1, order)),
                  full((1, order)),
                  pl.BlockSpec((order, ct), lambda c, s: (0, d // ct + c)),
                  pl.BlockSpec((order, ct), lambda c, s: (0, c)),
                  pl.BlockSpec((1, ct), lambda c, s: (0, c)),
                  _resident((4, rows, m), lambda c, s: (0, 0, 0))],
        out_specs=pl.BlockSpec((1, rows, ct), lambda c, s: (s, 0, c)),
        out_shape=jax.ShapeDtypeStruct((2 * nb - 1, rows, d), BF16),
        scratch_shapes=[pltpu.VMEM((n, order), BF16), pltpu.VMEM((2 * n, ct), BF16)],
        compiler_params=_cparams(2),
        name="hyena_filter",
    )(z, w0p, b0.reshape(1, order), w1.astype(BF16), b1.reshape(1, order), w2.astype(BF16),
      b2.reshape(1, order), freq.reshape(1, order), w3b, w3b, deltas, mats)


def _conv3_rows(x, w, b):
    n = x.shape[0]
    row = lax.broadcasted_iota(jnp.int32, (n, 1), 0)
    dn = jnp.where(row == 0, 0.0, pltpu.roll(x, 1, 0))
    up = jnp.where(row == n - 1, 0.0, pltpu.roll(x, n - 1, 0))
    return dn * w[0:1, :] + x * w[1:2, :] + up * w[2:3, :] + b


def _hy_conv_kernel(p0_ref, p1_ref, p2_ref, w0_ref, w1_ref, w2_ref, b0_ref, b1_ref, b2_ref, hb_ref,
                    g_ref, ma_ref, minv_ref, o_ref, x0_ref, v_ref, vf_ref, *, m, nb):
    i = pl.program_id(2)

    @pl.when(i == 0)
    def _():
        x0_ref[...] = _conv3_rows(p0_ref[...].astype(F32), w0_ref[...], b0_ref[...])
        x1 = _conv3_rows(p1_ref[...].astype(F32), w1_ref[...], b1_ref[...])
        v = _conv3_rows(p2_ref[...].astype(F32), w2_ref[...], b2_ref[...])
        vb = (v * x1).astype(BF16)
        v_ref[...] = vb
        for j in range(nb):
            vf_ref[j] = _dot(ma_ref[...], vb[j * m:(j + 1) * m, :])

    ct = o_ref.shape[1]
    acc_r = jnp.zeros((m, ct), F32)
    acc_i = jnp.zeros((m, ct), F32)
    acc_n = jnp.zeros((NYQ_ROWS, ct), F32)
    for j in range(nb):
        g = g_ref[i - j + nb - 1].astype(F32)
        vr = vf_ref[j, 0:m, :]
        vi = vf_ref[j, m:2 * m, :]
        gr = g[0:m, :]
        gi = g[m:2 * m, :]
        acc_r += gr * vr - gi * vi
        acc_i += gr * vi + gi * vr
        acc_n += g[2 * m:, :] * vf_ref[j, 2 * m:, :]
    spec = jnp.concatenate([acc_r, acc_i], axis=0).astype(BF16)
    t = lax.broadcasted_iota(jnp.int32, (m, 1), 0)
    sgn = (1 - 2 * jnp.bitwise_and(t, 1)).astype(F32) * (0.5 / m)
    y = _dot(minv_ref[...], spec) + sgn * acc_n[0:1, :]
    start = pl.multiple_of(i * m, m)
    v1 = v_ref[pl.ds(start, m), :].astype(F32)
    o_ref[...] = ((y + v1 * hb_ref[...]) * x0_ref[pl.ds(start, m), :]).astype(o_ref.dtype)


def _hy_conv(p, conv_w, conv_b, hy_bias, g, ma, minv, *, n, m, batch):
    d = hy_bias.shape[0]
    nb = n // m
    ct = _tile(d, 256)
    nct = d // ct
    rows = 2 * m + NYQ_ROWS
    pcol = lambda k: pl.BlockSpec((n, ct), lambda c, b, i: (b, k * nct + c))
    wcol = lambda k: pl.BlockSpec((3, ct), lambda c, b, i: (0, k * nct + c))
    bcol = lambda k: pl.BlockSpec((1, ct), lambda c, b, i: (0, k * nct + c))
    return pl.pallas_call(
        functools.partial(_hy_conv_kernel, m=m, nb=nb),
        grid=(nct, batch, nb),
        in_specs=[pcol(0), pcol(1), pcol(2), wcol(0), wcol(1), wcol(2), bcol(0), bcol(1), bcol(2),
                  pl.BlockSpec((1, ct), lambda c, b, i: (0, c)),
                  _resident((2 * nb - 1, rows, ct), lambda c, b, i: (0, 0, c)),
                  _resident((rows, m), lambda c, b, i: (0, 0)),
                  _resident((m, 2 * m), lambda c, b, i: (0, 0))],
        out_specs=pl.BlockSpec((m, ct), lambda c, b, i: (b * nb + i, c)),
        out_shape=jax.ShapeDtypeStruct((batch * n, d), BF16),
        scratch_shapes=[pltpu.VMEM((n, ct), F32), pltpu.VMEM((n, ct), BF16),
                        pltpu.VMEM((nb, rows, ct), F32)],
        compiler_params=_cparams(3),
        name="hyena_conv",
    )(p, p, p, conv_w, conv_w, conv_w, conv_b.reshape(1, 3 * d), conv_b.reshape(1, 3 * d),
      conv_b.reshape(1, 3 * d), hy_bias.reshape(1, d), g, ma, minv)


def _rope_tables(n):
    axis = ROPE_DIM // 2
    half = axis // 2
    inv = ROPE_THETA ** (-np.arange(0, axis, 2) / axis)
    t = np.arange(n)
    row, col = t // GRID_W, t % GRID_W
    cos = np.ones((n, HEAD_SLOT))
    sin_a = np.zeros((n, HEAD_SLOT))
    sin_b = np.zeros((n, HEAD_SLOT))
    for k, pos in enumerate((row, col)):
        ang = pos[:, None] * inv[None, :]
        base = QK_NOPE + k * axis
        cos[:, base:base + half] = np.cos(ang)
        cos[:, base + half:base + axis] = np.cos(ang)
        sin_a[:, base:base + half] = -np.sin(ang)
        sin_b[:, base + half:base + axis] = np.sin(ang)
    return jnp.asarray(cos.T, F32), jnp.asarray(sin_a.T, F32), jnp.asarray(sin_b.T, F32)


def _expand_keys(ckv_b, kpe_slot, wukv_ref, kng, k_ref, v_ref):
    kw = N_HEADS * HEAD_SLOT
    kv = _dot(ckv_b, wukv_ref[...])
    for h in range(N_HEADS):
        sl = slice(h * HEAD_SLOT, (h + 1) * HEAD_SLOT)
        x = kv[:, sl] + kpe_slot
        scale = lax.rsqrt(jnp.sum(x * x, axis=-1, keepdims=True) * (1.0 / QK_DIM) + EPS)
        k_ref[:, sl] = (x * scale * kng).astype(BF16)
    v_ref[...] = kv[:, kw:].T.astype(BF16)


def _rmsnorm_rows(x, g_col):
    ms = jnp.mean(x * x, axis=0, keepdims=True)
    return x * lax.rsqrt(ms + EPS) * g_col


def _rope_gains(g3, rope):
    if rope is None:
        return g3[:, 0:1], None, None
    cos, sin_a, sin_b = rope
    return g3[:, 0:1] * cos, g3[:, 1:2] * sin_a, g3[:, 2:3] * sin_b


def _partner_term(x, ga, gb):
    half = ROPE_DIM // 4
    return pltpu.roll(x, HEAD_SLOT - half, 0) * ga + pltpu.roll(x, half, 0) * gb


def _slab_norm_rope(x, gc, partner):
    scale = lax.rsqrt(jnp.sum(x * x, axis=0, keepdims=True) * (1.0 / QK_DIM) + EPS)
    y = x * gc
    if partner is not None:
        y = y + partner
    return y * scale


def _mla_prep_kernel(*refs, row_base, tpb, use_rope, emit_latent):
    (x_ref, mod_ref, g_ref, wcat_ref, qg_ref, kvg_ref, wuq_ref, wukv_ref, qng_ref, kng_ref), refs = refs[:10], refs[10:]
    if use_rope:
        (cos_ref, sa_ref, sb_ref), refs = refs[:3], refs[3:]
        rope = (cos_ref[...], sa_ref[...], sb_ref[...])
    else:
        rope = None
    q_ref, k_ref, v_ref = refs[:3]
    i = pl.program_id(0)
    d = x_ref.shape[1]
    r = _mod_row(i, row_base, tpb)
    h = _modnorm(x_ref[...], g_ref[...], _mod_chunk(mod_ref, r, SC1, d), _mod_chunk(mod_ref, r, SH1, d))
    at = _dot(wcat_ref[...], h.T.astype(BF16))
    q_rank = qg_ref.shape[0]
    kv_rank = kvg_ref.shape[0]
    qct = _rmsnorm_rows(at[:q_rank], qg_ref[...]).astype(BF16)
    ckvt = _rmsnorm_rows(at[q_rank:q_rank + kv_rank], kvg_ref[...])
    kpet = at[q_rank + kv_rank:]
    if emit_latent:
        refs[3][...] = ckvt.T
        refs[4][...] = kpet.T
    qt = _dot(wuq_ref[...], qct)
    qc, qa, qb = _rope_gains(qng_ref[...] * (QK_DIM ** -0.5 * math.log2(math.e)), rope)
    for hd in range(N_HEADS):
        sl = slice(hd * HEAD_SLOT, (hd + 1) * HEAD_SLOT)
        x = qt[sl]
        q_ref[sl, :] = _slab_norm_rope(x, qc, _partner_term(x, qa, qb) if use_rope else None).astype(BF16)
    kw = N_HEADS * HEAD_SLOT
    kvt = _dot(wukv_ref[...], ckvt.astype(BF16))
    v_ref[...] = kvt[kw:].astype(BF16)
    kc, ka, kb = _rope_gains(kng_ref[...], rope)
    partner = _partner_term(kpet, ka, kb) if use_rope else None
    for hd in range(N_HEADS):
        sl = slice(hd * HEAD_SLOT, (hd + 1) * HEAD_SLOT)
        k_ref[:, sl] = _slab_norm_rope(kvt[sl] + kpet, kc, partner).T.astype(BF16)


def _mla_prep(x, mod, g, wcat_t, qg, kvg, wuq_t, wukv_t, qng3, kng3, rope_t, *, row_base, rows_per_batch,
              emit_latent):
    n, d = x.shape
    seq = n if rows_per_batch is None else rows_per_batch
    tm = min(512, seq)
    tpb = None if rows_per_batch is None else rows_per_batch // tm
    kw = N_HEADS * HEAD_SLOT
    vw = N_HEADS * V_DIM
    q_rank, kv_rank = qg.shape[0], kvg.shape[0]
    full = lambda a: pl.BlockSpec(a.shape, lambda i: (0,) * a.ndim)
    args = [x, mod, g.reshape(1, d), wcat_t, qg.reshape(-1, 1), kvg.reshape(-1, 1), wuq_t, wukv_t, qng3, kng3]
    in_specs = [pl.BlockSpec((tm, d), lambda i: (i, 0))] + [full(a) for a in args[1:]]
    if rope_t is not None:
        pt = rows_per_batch // tm
        args += list(rope_t)
        in_specs += [pl.BlockSpec((HEAD_SLOT, tm), lambda i: (0, i % pt))] * 3
    row = lambda w: pl.BlockSpec((tm, w), lambda i: (i, 0))
    col = lambda w: pl.BlockSpec((w, tm), lambda i: (0, i))
    out_specs = [col(kw), row(kw), col(vw)]
    out_shape = [jax.ShapeDtypeStruct((kw, n), BF16), jax.ShapeDtypeStruct((n, kw), BF16),
                 jax.ShapeDtypeStruct((vw, n), BF16)]
    if emit_latent:
        out_specs += [row(kv_rank), row(HEAD_SLOT)]
        out_shape += [jax.ShapeDtypeStruct((n, kv_rank), F32), jax.ShapeDtypeStruct((n, HEAD_SLOT), F32)]
    return pl.pallas_call(
        functools.partial(_mla_prep_kernel, row_base=row_base, tpb=tpb, use_rope=rope_t is not None,
                          emit_latent=emit_latent),
        grid=(n // tm,),
        in_specs=in_specs, out_specs=out_specs, out_shape=out_shape,
        compiler_params=_cparams(1),
        name="mla_prep",
    )(*args)


def _mla_cache_kernel(ckv_ref, kpe_ref, wukv_ref, kng_ref, k_ref, v_ref):
    _expand_keys(ckv_ref[...].astype(BF16), kpe_ref[...], wukv_ref, kng_ref[...], k_ref, v_ref)


def _mla_cache(ckv, kpe_slot, wukv, kng):
    n = ckv.shape[0]
    tm = min(512, n)
    kw = N_HEADS * HEAD_SLOT
    vw = N_HEADS * V_DIM
    row = lambda w: pl.BlockSpec((tm, w), lambda i: (i, 0))
    return pl.pallas_call(
        _mla_cache_kernel,
        grid=(n // tm,),
        in_specs=[row(ckv.shape[1]), row(HEAD_SLOT), pl.BlockSpec(wukv.shape, lambda i: (0, 0)),
                  pl.BlockSpec(kng.shape, lambda i: (0, 0))],
        out_specs=[row(kw), pl.BlockSpec((vw, tm), lambda i: (0, i))],
        out_shape=[jax.ShapeDtypeStruct((n, kw), BF16), jax.ShapeDtypeStruct((vw, n), BF16)],
        compiler_params=_cparams(1),
        name="mla_cache_keys",
    )(ckv, kpe_slot, wukv, kng)


KEY_CHUNK = 256


def _attn_kernel(*refs, has_cache, heads):
    if has_cache:
        q_ref, k_ref, v_ref, kc_ref, vc_ref, o_ref = refs
    else:
        q_ref, k_ref, v_ref, o_ref = refs
    n = k_ref.shape[0]
    outs = []
    for hh in range(heads):
        sl = slice(hh * HEAD_SLOT, (hh + 1) * HEAD_SLOT)
        vs = slice(hh * V_DIM, (hh + 1) * V_DIM)
        q = q_ref[sl, :]
        pieces = [(k_ref, v_ref, c, min(KEY_CHUNK, n - c)) for c in range(0, n, KEY_CHUNK)]
        if has_cache:
            pieces.append((kc_ref, vc_ref, 0, kc_ref.shape[0]))
        sts = [_dot(kr[c:c + w, sl], q) for kr, _, c, w in pieces]
        mx = functools.reduce(jnp.maximum, [jnp.max(st, axis=0, keepdims=True) for st in sts])
        pts = [jnp.exp2(st - mx) for st in sts]
        l = functools.reduce(jnp.add, [jnp.sum(pt, axis=0, keepdims=True) for pt in pts])
        ot = functools.reduce(jnp.add, [_dot(vr[vs, c:c + w], pt.astype(BF16))
                                        for (_, vr, c, w), pt in zip(pieces, pts)])
        outs.append(ot / l)
    o_ref[...] = jnp.concatenate(outs, axis=0).T.astype(o_ref.dtype)


def _attention(q, k, v, kc, vc, *, batch, n, past):
    has_cache = kc is not None
    tq = min(512, n)
    heads = 2 if n > tq else N_HEADS
    nq = n // tq
    groups = N_HEADS // heads
    in_specs = [pl.BlockSpec((heads * HEAD_SLOT, tq), lambda b, j, i: (j, b * nq + i)),
                pl.BlockSpec((n, heads * HEAD_SLOT), lambda b, j, i: (b, j)),
                pl.BlockSpec((heads * V_DIM, n), lambda b, j, i: (j, b))]
    args = [q, k, v]
    if has_cache:
        in_specs += [pl.BlockSpec((past, heads * HEAD_SLOT), lambda b, j, i: (b, j)),
                     pl.BlockSpec((heads * V_DIM, past), lambda b, j, i: (j, b))]
        args += [kc, vc]
    return pl.pallas_call(
        functools.partial(_attn_kernel, has_cache=has_cache, heads=heads),
        grid=(batch, groups, nq),
        in_specs=in_specs,
        out_specs=pl.BlockSpec((tq, heads * V_DIM), lambda b, j, i: (b * nq + i, j)),
        out_shape=jax.ShapeDtypeStruct((batch * n, N_HEADS * V_DIM), BF16),
        compiler_params=_cparams(3),
        name="attention",
    )(*args)


def _pad_heads(w, width):
    k = w.shape[0]
    w = w.reshape(k, N_HEADS, width)
    return jnp.pad(w, ((0, 0), (0, 0), (0, HEAD_SLOT - width))).reshape(k, N_HEADS * HEAD_SLOT)


def _slot_gain(g):
    return jnp.pad(g, (0, HEAD_SLOT - QK_DIM)).reshape(1, HEAD_SLOT)


def _gain_columns(g_row):
    g = g_row.reshape(HEAD_SLOT)
    half = ROPE_DIM // 4
    return jnp.stack([g, jnp.roll(g, -half), jnp.roll(g, half)], axis=1)


def kernel(x_prompt, x_sample, cache_ckv, cache_kpe, c, c_ctx, norm1_g, norm2_g, mod_w, mod_b, mlp_w1, mlp_w2, sc_w_in, sc_conv_w, sc_conv_b, sc_w_out, mla_w_dq, mla_q_norm_g, mla_w_uq, mla_w_dkv, mla_kv_norm_g, mla_w_ukv, mla_qn_g, mla_kn_g, mla_w_o, hy_w_in, hy_conv_w, hy_conv_b, hy_f_w0, hy_f_b0, hy_f_w1, hy_f_b1, hy_f_w2, hy_f_b2, hy_f_w3, hy_sin_freq, hy_bias, hy_w_out):
    batch, seq, d = x_prompt.shape
    dec_batch, dec_seq, _ = x_sample.shape
    depth = mod_w.shape[0]
    n_mla = mla_w_dq.shape[0]
    past = cache_ckv.shape[2]
    kv_rank = mla_kv_norm_g.shape[1]
    assert 1 + dec_batch <= MOD_ROWS

    cond = jnp.zeros((MOD_ROWS, d), F32).at[0].set(c_ctx).at[1:1 + dec_batch].set(c)
    mods = _modulation(cond, mod_w, mod_b)

    w1 = mlp_w1.astype(BF16)
    w2 = mlp_w2.astype(BF16)
    sc_in = sc_w_in.astype(BF16)
    sc_out = sc_w_out.astype(BF16)
    hy_in = hy_w_in.astype(BF16)
    hy_out = hy_w_out.astype(BF16)
    mla_o = mla_w_o.astype(BF16)

    streams = [
        dict(x=x_prompt.reshape(batch * seq, d), row_base=0, rpb=None, seq=seq, nseq=batch, ctx=True),
        dict(x=x_sample.reshape(dec_batch * dec_seq, d), row_base=1, rpb=dec_seq, seq=dec_seq,
             nseq=dec_batch, ctx=False),
    ]
    hy_consts = {}
    new_ckv, new_kpe = [], []

    for i in range(depth):
        kind, j = i % 3, i // 3
        mod = mods[i]
        for st in streams:
            x = st["x"]
            rb, rpb, n = st["row_base"], st["rpb"], st["seq"]
            if kind == 0:
                p = _nm_matmul(x, mod, norm1_g[i], sc_in, j, row_base=rb, rows_per_batch=rpb)
                x = _sconv_out(p, sc_conv_w[j], sc_conv_b[j], sc_out, j, x, mod, seq_len=n,
                               row_base=rb, rows_per_batch=rpb)
            elif kind == 1:
                pe = mla_w_dkv[j][:, kv_rank:]
                pe_slot = jnp.pad(pe, ((0, 0), (QK_NOPE, HEAD_SLOT - QK_DIM)))
                wcat = jnp.concatenate([mla_w_dq[j], mla_w_dkv[j][:, :kv_rank], pe_slot], axis=1).astype(BF16)
                wuq = _pad_heads(mla_w_uq[j], QK_DIM).astype(BF16)
                ukv = mla_w_ukv[j].reshape(kv_rank, N_HEADS, QK_NOPE + V_DIM)
                wukv = jnp.concatenate(
                    [_pad_heads(ukv[:, :, :QK_NOPE].reshape(kv_rank, -1), QK_NOPE),
                     ukv[:, :, QK_NOPE:].reshape(kv_rank, -1)], axis=1).astype(BF16)
                qng, kng = _slot_gain(mla_qn_g[j]), _slot_gain(mla_kn_g[j])
                rope = None if st["ctx"] else _rope_tables(n)
                outs = _mla_prep(x, mod, norm1_g[i], wcat.T, mla_q_norm_g[j], mla_kv_norm_g[j], wuq.T, wukv.T,
                                 _gain_columns(qng), _gain_columns(kng), rope, row_base=rb,
                                 rows_per_batch=rpb, emit_latent=st["ctx"])
                q, k, v = outs[:3]
                if st["ctx"]:
                    new_ckv.append(outs[3].reshape(batch, seq, kv_rank))
                    new_kpe.append(outs[4][:, QK_NOPE:QK_DIM].reshape(batch, seq, ROPE_DIM))
                    kc = vc = None
                else:
                    ck = cache_ckv[:, j].reshape(dec_batch * past, kv_rank)
                    cp = jnp.pad(cache_kpe[:, j].reshape(dec_batch * past, ROPE_DIM),
                                 ((0, 0), (QK_NOPE, HEAD_SLOT - QK_DIM)))
                    kc, vc = _mla_cache(ck, cp, wukv, kng)
                o = _attention(q, k, v, kc, vc, batch=st["nseq"], n=n, past=past)
                x = _mm_res(o, mla_o, j, x, mod, row_base=rb, rows_per_batch=rpb)
            else:
                m = min(HY_BLOCK, n)
                if m not in hy_consts:
                    hy_consts[m] = _dft_constants(m)
                mats, minv = hy_consts[m]
                g = _hy_filter(n, m, hy_f_w0[j], hy_f_b0[j], hy_f_w1[j], hy_f_b1[j], hy_f_w2[j],
                               hy_f_b2[j], hy_f_w3[j], hy_sin_freq[j], mats)
                p = _nm_matmul(x, mod, norm1_g[i], hy_in, j, row_base=rb, rows_per_batch=rpb)
                z = _hy_conv(p, hy_conv_w[j], hy_conv_b[j], hy_bias[j], g, mats[FWD0], minv, n=n, m=m,
                             batch=st["nseq"])
                x = _mm_res(z, hy_out, j, x, mod, row_base=rb, rows_per_batch=rpb)
            x = _mlp(x, mod, norm2_g[i], w1, w2, i, row_base=rb, rows_per_batch=rpb)
            st["x"] = x

    y_prompt = streams[0]["x"].reshape(batch, seq, d)
    y_sample = streams[1]["x"].reshape(dec_batch, dec_seq, d)
    return (y_prompt, y_sample, jnp.stack(new_ckv, axis=1), jnp.stack(new_kpe, axis=1))
```

```python
import functools
import math

import numpy as np
import jax
import jax.numpy as jnp
from jax import lax
from jax.experimental import pallas as pl
from jax.experimental.pallas import tpu as pltpu

F32 = jnp.float32
BF16 = jnp.bfloat16

EPS = 1e-6
MOD_CHUNKS = 6
N_HEADS = 16
QK_NOPE = 64
ROPE_DIM = 32
QK_DIM = QK_NOPE + ROPE_DIM
V_DIM = 64
HEAD_SLOT = 128
GRID_W = 64
ROPE_THETA = 10000.0
HY_TARGET = 1e-2
HY_FAST = 0.3
HY_SLOW = 1.5
HY_BLOCK = 1024
NYQ_ROWS = 16
MOD_ROWS = 8

V7X_VMEM_BYTES = 64 * 1024 * 1024
VMEM_LIMIT = V7X_VMEM_BYTES - 8 * 1024 * 1024

SH1, SC1, G1, SH2, SC2, G2 = range(6)


def _tile(n, pref, align=128):
    if n <= pref:
        return n
    t = pref - pref % align
    while n % t:
        t -= align
    return t


def _cparams(n_axes):
    return pltpu.CompilerParams(dimension_semantics=("arbitrary",) * n_axes,
                                vmem_limit_bytes=VMEM_LIMIT)


def _mod_chunk(mod_ref, r, k, d):
    return mod_ref[pl.ds(r, 1), k * d:(k + 1) * d]


def _mod_row(i, row_base, tiles_per_batch):
    if tiles_per_batch is None:
        return row_base
    return row_base + i // tiles_per_batch


def _modnorm(x, g, sc, sh):
    ms = jnp.mean(x * x, axis=-1, keepdims=True)
    return x * lax.rsqrt(ms + EPS) * (g * (1.0 + sc)) + sh


def _dot(a, b):
    return jnp.dot(a, b, preferred_element_type=F32)


def _mod_kernel(c_ref, w_ref, b_ref, o_ref):
    c = c_ref[...]
    s = (c * jax.nn.sigmoid(c)).astype(BF16)
    o_ref[0] = _dot(s, w_ref[0].astype(BF16)) + b_ref[0]


def _modulation(cond, mod_w, mod_b):
    depth, d, n6 = mod_w.shape
    tn = _tile(n6, 1536)
    return pl.pallas_call(
        _mod_kernel,
        grid=(depth, n6 // tn),
        in_specs=[pl.BlockSpec((MOD_ROWS, d), lambda l, j: (0, 0)),
                  pl.BlockSpec((1, d, tn), lambda l, j: (l, 0, j)),
                  pl.BlockSpec((1, 1, tn), lambda l, j: (l, 0, j))],
        out_specs=pl.BlockSpec((1, MOD_ROWS, tn), lambda l, j: (l, 0, j)),
        out_shape=jax.ShapeDtypeStruct((depth, MOD_ROWS, n6), F32),
        compiler_params=_cparams(2),
        name="modulation",
    )(cond, mod_w, mod_b.reshape(depth, 1, n6))


def _resident(shape, index_map):
    return pl.BlockSpec(shape, index_map, pipeline_mode=pl.Buffered(1))


def _mlp_kernel(*refs, row_base, tpb, tf, mixer_tail):
    if mixer_tail:
        z_ref, wo_ref, refs = refs[0], refs[1], refs[2:]
    x_ref, mod_ref, g_ref, w1_ref, w2_ref, o_ref = refs
    d = x_ref.shape[1]
    r = _mod_row(pl.program_id(0), row_base, tpb)
    x = x_ref[...]
    if mixer_tail:
        x = x + _mod_chunk(mod_ref, r, G1, d) * _dot(z_ref[...], wo_ref[...])
    h = _modnorm(x, g_ref[...], _mod_chunk(mod_ref, r, SC2, d), _mod_chunk(mod_ref, r, SH2, d)).astype(BF16)
    acc = None
    for c in range(0, w1_ref.shape[1], tf):
        a = jnp.maximum(_dot(h, w1_ref[:, c:c + tf]), 0.0)
        y = _dot((a * a).astype(BF16), w2_ref[c:c + tf, :])
        acc = y if acc is None else acc + y
    o_ref[...] = x + _mod_chunk(mod_ref, r, G2, d) * acc


def _mlp(x, mod, g, w1, w2, layer, *, row_base, rows_per_batch, tail=None):
    n, d = x.shape
    dff = w1.shape[2]
    tm = min(1024, n if rows_per_batch is None else rows_per_batch)
    tpb = None if rows_per_batch is None else rows_per_batch // tm
    args = [x, mod, g.reshape(1, d), w1, w2]
    in_specs = [pl.BlockSpec((tm, d), lambda i: (i, 0)),
                pl.BlockSpec(mod.shape, lambda i: (0, 0)),
                pl.BlockSpec((1, d), lambda i: (0, 0)),
                _resident((None, d, dff), lambda i: (layer, 0, 0)),
                _resident((None, dff, d), lambda i: (layer, 0, 0))]
    if tail is not None:
        z, w_out, tail_layer = tail
        k = z.shape[1]
        args = [z, w_out] + args
        in_specs = [pl.BlockSpec((tm, k), lambda i: (i, 0)),
                    _resident((None, k, d), lambda i: (tail_layer, 0, 0))] + in_specs
    return pl.pallas_call(
        functools.partial(_mlp_kernel, row_base=row_base, tpb=tpb, tf=_tile(dff, 1024),
                          mixer_tail=tail is not None),
        grid=(n // tm,),
        in_specs=in_specs,
        out_specs=pl.BlockSpec((tm, d), lambda i: (i, 0)),
        out_shape=jax.ShapeDtypeStruct((n, d), F32),
        compiler_params=_cparams(1),
        name="mlp",
    )(*args)


def _nm_matmul_kernel(x_ref, mod_ref, g_ref, w_ref, o_ref, *, row_base, tpb, tn):
    d = x_ref.shape[1]
    r = _mod_row(pl.program_id(0), row_base, tpb)
    h = _modnorm(x_ref[...], g_ref[...], _mod_chunk(mod_ref, r, SC1, d), _mod_chunk(mod_ref, r, SH1, d))
    h = h.astype(BF16)
    for c in range(0, o_ref.shape[1], tn):
        o_ref[:, c:c + tn] = _dot(h, w_ref[:, c:c + tn]).astype(o_ref.dtype)


def _nm_matmul(x, mod, g, w, layer, *, row_base, rows_per_batch):
    n, d = x.shape
    nout = w.shape[2]
    tm = min(1024, n if rows_per_batch is None else rows_per_batch)
    tpb = None if rows_per_batch is None else rows_per_batch // tm
    return pl.pallas_call(
        functools.partial(_nm_matmul_kernel, row_base=row_base, tpb=tpb, tn=_tile(nout, 1024)),
        grid=(n // tm,),
        in_specs=[pl.BlockSpec((tm, d), lambda i: (i, 0)),
                  pl.BlockSpec(mod.shape, lambda i: (0, 0)),
                  pl.BlockSpec((1, d), lambda i: (0, 0)),
                  _resident((None, d, nout), lambda i: (layer, 0, 0))],
        out_specs=pl.BlockSpec((tm, nout), lambda i: (i, 0)),
        out_shape=jax.ShapeDtypeStruct((n, nout), BF16),
        compiler_params=_cparams(1),
        name="norm_proj",
    )(x, mod, g.reshape(1, d), w)


HALO = 16


def _sconv_kernel(x_ref, xp_ref, xn_ref, mod_ref, g_ref, win_ref, cw_ref, cb_ref, wout_ref, o_ref, *,
                  seq_len, row_base, tpb):
    i = pl.program_id(0)
    tm, d = x_ref.shape
    r = _mod_row(i, row_base, tpb)
    x = x_ref[...]
    xcat = jnp.concatenate([xp_ref[...], x, xn_ref[...]], axis=0)
    h = _modnorm(xcat, g_ref[...], _mod_chunk(mod_ref, r, SC1, d), _mod_chunk(mod_ref, r, SH1, d)).astype(BF16)
    cx = _dot(h, win_ref[:, d:2 * d]) * _dot(h, win_ref[:, 2 * d:3 * d])
    rows = tm + 2 * HALO
    inner = slice(HALO, HALO + tm)
    row = lax.broadcasted_iota(jnp.int32, (tm, 1), 0)
    pos = jnp.bitwise_and(i * tm + row, seq_len - 1)
    dn = jnp.where(pos == 0, 0.0, pltpu.roll(cx, 1, 0)[inner])
    up = jnp.where(pos == seq_len - 1, 0.0, pltpu.roll(cx, rows - 1, 0)[inner])
    cw = cw_ref[...]
    u = dn * cw[0:1, :] + cx[inner] * cw[1:2, :] + up * cw[2:3, :] + cb_ref[...]
    z = (_dot(h[inner], win_ref[:, 0:d]) * u).astype(BF16)
    o_ref[...] = x + _mod_chunk(mod_ref, r, G1, d) * _dot(z, wout_ref[...])


def _sconv(x, mod, g, w_in, conv_w, conv_b, w_out, layer, *, seq_len, row_base, rows_per_batch):
    n, d = x.shape
    assert seq_len & (seq_len - 1) == 0
    tm = min(1024, n if rows_per_batch is None else rows_per_batch)
    tpb = None if rows_per_batch is None else rows_per_batch // tm
    hb = tm // HALO
    last = n // HALO - 1
    return pl.pallas_call(
        functools.partial(_sconv_kernel, seq_len=seq_len, row_base=row_base, tpb=tpb),
        grid=(n // tm,),
        in_specs=[pl.BlockSpec((tm, d), lambda i: (i, 0)),
                  pl.BlockSpec((HALO, d), lambda i: (jnp.maximum(i * hb - 1, 0), 0)),
                  pl.BlockSpec((HALO, d), lambda i: (jnp.minimum((i + 1) * hb, last), 0)),
                  pl.BlockSpec(mod.shape, lambda i: (0, 0)),
                  pl.BlockSpec((1, d), lambda i: (0, 0)),
                  _resident((None, d, 3 * d), lambda i: (layer, 0, 0)),
                  pl.BlockSpec((3, d), lambda i: (0, 0)),
                  pl.BlockSpec((1, d), lambda i: (0, 0)),
                  _resident((None, d, d), lambda i: (layer, 0, 0))],
        out_specs=pl.BlockSpec((tm, d), lambda i: (i, 0)),
        out_shape=jax.ShapeDtypeStruct((n, d), F32),
        compiler_params=_cparams(1),
        name="sconv",
    )(x, x, x, mod, g.reshape(1, d), w_in, conv_w, conv_b.reshape(1, d), w_out)


def _dft_constants(m):
    k = np.arange(m)[:, None]
    s = np.arange(m)[None, :]
    ang = np.pi * ((k * s) % (2 * m)) / m
    rows = 2 * m + NYQ_ROWS
    alt_k = (-1.0) ** np.arange(m)[:, None]
    alt_s = (-1.0) ** np.arange(m)

    def mat(re, im, drop_first):
        out = np.zeros((rows, m))
        out[:m], out[m:2 * m], out[2 * m] = re, im, alt_s
        if drop_first:
            out[:, 0] = 0.0
        return out

    cos, sin = np.cos(ang), np.sin(ang)
    mats = np.stack([mat(cos, -sin, False), mat(alt_k * cos, -alt_k * sin, True),
                     mat(cos, sin, False), mat(alt_k * cos, alt_k * sin, True)])
    ck = np.full((1, m), 2.0)
    ck[0, 0] = 1.0
    minv = np.concatenate([ck * cos.T, -ck * sin.T], axis=1) / (2 * m)
    return jnp.asarray(mats, F32).astype(BF16), jnp.asarray(minv, F32).astype(BF16)


FWD0, FWD1, BWD0, BWD1 = range(4)


def _filter_features(n, n_emb_pad):
    bands = (33 - 1) // 2
    t01 = np.linspace(0.0, 1.0, n)[:, None]
    w = 2.0 * np.pi * np.arange(n)[:, None] / n
    fb = np.linspace(1e-4, bands - 1, bands)[None, :]
    z = np.concatenate([t01, np.cos(fb * w), -np.sin(fb * w)], axis=-1)
    out = np.zeros((n, n_emb_pad))
    out[:, :z.shape[1]] = z
    return jnp.asarray(out, F32)


def _hy_filter_kernel(z_ref, w0_ref, b0_ref, w1_ref, b1_ref, w2_ref, b2_ref, fr_ref, w3b_ref, w3f_ref,
                      dl_ref, mats_ref, g_ref, a3_ref, f_ref, *, n, m):
    c = pl.program_id(0)
    dstep = pl.program_id(1)
    nb = n // m

    @pl.when(jnp.logical_and(c == 0, dstep == 0))
    def _():
        fr = fr_ref[...]
        a = jnp.sin(fr * (_dot(z_ref[...].astype(BF16), w0_ref[...]) + b0_ref[...]))
        a = jnp.sin(fr * (_dot(a.astype(BF16), w1_ref[...]) + b1_ref[...]))
        a = jnp.sin(fr * (_dot(a.astype(BF16), w2_ref[...]) + b2_ref[...]))
        a3_ref[...] = a.astype(BF16)

    @pl.when(dstep == 0)
    def _():
        t = lax.broadcasted_iota(jnp.int32, (n, 1), 0)
        decay = jnp.exp(-(t.astype(F32) / (n - 1.0)) * dl_ref[...])
        a3 = a3_ref[...]
        hb = jnp.where(t == 0, 0.0, _dot(a3, w3b_ref[...]) * decay)
        hf = _dot(a3, w3f_ref[...]) * decay
        norm = (jnp.sum(jnp.abs(hb), axis=0, keepdims=True)
                + jnp.sum(jnp.abs(hf), axis=0, keepdims=True))
        f_ref[0:n, :] = (hb / norm).astype(BF16)
        f_ref[n:2 * n, :] = (hf / norm).astype(BF16)

    d = dstep - (nb - 1)
    ia = jnp.where(d >= 0, FWD0, BWD0)
    ra = jnp.where(d >= 0, n + d * m, -d * m)
    ib = jnp.where(d >= 1, FWD1, jnp.where(d == 0, BWD0, BWD1))
    rb = jnp.where(d >= 1, n + (d - 1) * m, jnp.where(d == 0, 0, (-d - 1) * m))
    ra = pl.multiple_of(ra, m)
    rb = pl.multiple_of(rb, m)
    g = _dot(mats_ref[ia], f_ref[pl.ds(ra, m), :]) + _dot(mats_ref[ib], f_ref[pl.ds(rb, m), :])
    g_ref[0] = g.astype(g_ref.dtype)


def _hy_filter(n, m, w0, b0, w1, b1, w2, b2, w3, freq, mats):
    d = w3.shape[1] // 2
    order = w1.shape[0]
    nb = n // m
    ct = _tile(d, 256)
    emb_pad = 128
    z = _filter_features(n, emb_pad)
    w0p = jnp.zeros((emb_pad, order), F32).at[:w0.shape[0]].set(w0).astype(BF16)
    lo, hi = math.log(HY_TARGET) / HY_SLOW, math.log(HY_TARGET) / HY_FAST
    deltas = jnp.asarray(np.abs(np.linspace(lo, hi, d))[None, :], F32)
    w3b = w3.astype(BF16)
    rows = 2 * m + NYQ_ROWS
    full = lambda shp: pl.BlockSpec(shp, lambda c, s: (0,) * len(shp))
    return pl.pallas_call(
        functools.partial(_hy_filter_kernel, n=n, m=m),
        grid=(d // ct, 2 * nb - 1),
        in_specs=[full((n, emb_pad)), full((emb_pad, order)), full((1, order)),
                  full((order, order)), full((1, order)), full((order, order)), full((1, order)),
                  full((1, order)),
                  pl.BlockSpec((order, ct), lambda c, s: (0, d // ct + c)),
                  pl.BlockSpec((order, ct), lambda c, s: (0, c)),
                  pl.BlockSpec((1, ct), lambda c, s: (0, c)),
                  _resident((4, rows, m), lambda c, s: (0, 0, 0))],
        out_specs=pl.BlockSpec((1, rows, ct), lambda c, s: (s, 0, c)),
        out_shape=jax.ShapeDtypeStruct((2 * nb - 1, rows, d), BF16),
        scratch_shapes=[pltpu.VMEM((n, order), BF16), pltpu.VMEM((2 * n, ct), BF16)],
        compiler_params=_cparams(2),
        name="hyena_filter",
    )(z, w0p, b0.reshape(1, order), w1.astype(BF16), b1.reshape(1, order), w2.astype(BF16),
      b2.reshape(1, order), freq.reshape(1, order), w3b, w3b, deltas, mats)


def _conv3_rows(x, w, b):
    n = x.shape[0]
    row = lax.broadcasted_iota(jnp.int32, (n, 1), 0)
    dn = jnp.where(row == 0, 0.0, pltpu.roll(x, 1, 0))
    up = jnp.where(row == n - 1, 0.0, pltpu.roll(x, n - 1, 0))
    return dn * w[0:1, :] + x * w[1:2, :] + up * w[2:3, :] + b


def _hy_conv_kernel(p0_ref, p1_ref, p2_ref, w0_ref, w1_ref, w2_ref, b0_ref, b1_ref, b2_ref, hb_ref,
                    g_ref, ma_ref, minv_ref, o_ref, x0_ref, v_ref, vf_ref, *, m, nb):
    i = pl.program_id(2)

    @pl.when(i == 0)
    def _():
        x0_ref[...] = _conv3_rows(p0_ref[...].astype(F32), w0_ref[...], b0_ref[...])
        x1 = _conv3_rows(p1_ref[...].astype(F32), w1_ref[...], b1_ref[...])
        v = _conv3_rows(p2_ref[...].astype(F32), w2_ref[...], b2_ref[...])
        vb = (v * x1).astype(BF16)
        v_ref[...] = vb
        for j in range(nb):
            vf_ref[j] = _dot(ma_ref[...], vb[j * m:(j + 1) * m, :])

    ct = o_ref.shape[1]
    acc_r = jnp.zeros((m, ct), F32)
    acc_i = jnp.zeros((m, ct), F32)
    acc_n = jnp.zeros((NYQ_ROWS, ct), F32)
    for j in range(nb):
        g = g_ref[i - j + nb - 1].astype(F32)
        vr = vf_ref[j, 0:m, :]
        vi = vf_ref[j, m:2 * m, :]
        gr = g[0:m, :]
        gi = g[m:2 * m, :]
        acc_r += gr * vr - gi * vi
        acc_i += gr * vi + gi * vr
        acc_n += g[2 * m:, :] * vf_ref[j, 2 * m:, :]
    spec = jnp.concatenate([acc_r, acc_i], axis=0).astype(BF16)
    t = lax.broadcasted_iota(jnp.int32, (m, 1), 0)
    sgn = (1 - 2 * jnp.bitwise_and(t, 1)).astype(F32) * (0.5 / m)
    y = _dot(minv_ref[...], spec) + sgn * acc_n[0:1, :]
    start = pl.multiple_of(i * m, m)
    v1 = v_ref[pl.ds(start, m), :].astype(F32)
    o_ref[...] = ((y + v1 * hb_ref[...]) * x0_ref[pl.ds(start, m), :]).astype(o_ref.dtype)


def _hy_conv(p, conv_w, conv_b, hy_bias, g, ma, minv, *, n, m, batch):
    d = hy_bias.shape[0]
    nb = n // m
    ct = _tile(d, 256)
    nct = d // ct
    rows = 2 * m + NYQ_ROWS
    pcol = lambda k: pl.BlockSpec((n, ct), lambda c, b, i: (b, k * nct + c))
    wcol = lambda k: pl.BlockSpec((3, ct), lambda c, b, i: (0, k * nct + c))
    bcol = lambda k: pl.BlockSpec((1, ct), lambda c, b, i: (0, k * nct + c))
    return pl.pallas_call(
        functools.partial(_hy_conv_kernel, m=m, nb=nb),
        grid=(nct, batch, nb),
        in_specs=[pcol(0), pcol(1), pcol(2), wcol(0), wcol(1), wcol(2), bcol(0), bcol(1), bcol(2),
                  pl.BlockSpec((1, ct), lambda c, b, i: (0, c)),
                  _resident((2 * nb - 1, rows, ct), lambda c, b, i: (0, 0, c)),
                  _resident((rows, m), lambda c, b, i: (0, 0)),
                  _resident((m, 2 * m), lambda c, b, i: (0, 0))],
        out_specs=pl.BlockSpec((m, ct), lambda c, b, i: (b * nb + i, c)),
        out_shape=jax.ShapeDtypeStruct((batch * n, d), BF16),
        scratch_shapes=[pltpu.VMEM((n, ct), F32), pltpu.VMEM((n, ct), BF16),
                        pltpu.VMEM((nb, rows, ct), F32)],
        compiler_params=_cparams(3),
        name="hyena_conv",
    )(p, p, p, conv_w, conv_w, conv_w, conv_b.reshape(1, 3 * d), conv_b.reshape(1, 3 * d),
      conv_b.reshape(1, 3 * d), hy_bias.reshape(1, d), g, ma, minv)


def _rope_tables(n):
    axis = ROPE_DIM // 2
    half = axis // 2
    inv = ROPE_THETA ** (-np.arange(0, axis, 2) / axis)
    t = np.arange(n)
    row, col = t // GRID_W, t % GRID_W
    cos = np.ones((n, HEAD_SLOT))
    sin_a = np.zeros((n, HEAD_SLOT))
    sin_b = np.zeros((n, HEAD_SLOT))
    for k, pos in enumerate((row, col)):
        ang = pos[:, None] * inv[None, :]
        base = QK_NOPE + k * axis
        cos[:, base:base + half] = np.cos(ang)
        cos[:, base + half:base + axis] = np.cos(ang)
        sin_a[:, base:base + half] = -np.sin(ang)
        sin_b[:, base + half:base + axis] = np.sin(ang)
    return jnp.asarray(cos.T, F32), jnp.asarray(sin_a.T, F32), jnp.asarray(sin_b.T, F32)


def _expand_keys(ckv_b, kpe_slot, wukv_ref, kng, k_ref, v_ref):
    kw = N_HEADS * HEAD_SLOT
    kv = _dot(ckv_b, wukv_ref[...])
    for h in range(N_HEADS):
        sl = slice(h * HEAD_SLOT, (h + 1) * HEAD_SLOT)
        x = kv[:, sl] + kpe_slot
        scale = lax.rsqrt(jnp.sum(x * x, axis=-1, keepdims=True) * (1.0 / QK_DIM) + EPS)
        k_ref[:, sl] = (x * scale * kng).astype(BF16)
    v_ref[...] = kv[:, kw:].T.astype(BF16)


def _rmsnorm_rows(x, g_col):
    ms = jnp.mean(x * x, axis=0, keepdims=True)
    return x * lax.rsqrt(ms + EPS) * g_col


def _rope_gains(g3, rope):
    if rope is None:
        return g3[:, 0:1], None, None
    cos, sin_a, sin_b = rope
    return g3[:, 0:1] * cos, g3[:, 1:2] * sin_a, g3[:, 2:3] * sin_b


def _partner_term(x, ga, gb):
    half = ROPE_DIM // 4
    return pltpu.roll(x, HEAD_SLOT - half, 0) * ga + pltpu.roll(x, half, 0) * gb


def _slab_norm_rope(x, gc, partner):
    scale = lax.rsqrt(jnp.sum(x * x, axis=0, keepdims=True) * (1.0 / QK_DIM) + EPS)
    y = x * gc
    if partner is not None:
        y = y + partner
    return y * scale


def _mla_prep_kernel(*refs, row_base, tpb, use_rope, emit_latent):
    (x_ref, mod_ref, g_ref, wcat_ref, qg_ref, kvg_ref, wuq_ref, wukv_ref, qng_ref, kng_ref), refs = refs[:10], refs[10:]
    if use_rope:
        (cos_ref, sa_ref, sb_ref), refs = refs[:3], refs[3:]
        rope = (cos_ref[...], sa_ref[...], sb_ref[...])
    else:
        rope = None
    q_ref, k_ref, v_ref = refs[:3]
    i = pl.program_id(0)
    d = x_ref.shape[1]
    r = _mod_row(i, row_base, tpb)
    h = _modnorm(x_ref[...], g_ref[...], _mod_chunk(mod_ref, r, SC1, d), _mod_chunk(mod_ref, r, SH1, d))
    at = _dot(wcat_ref[...], h.T.astype(BF16))
    q_rank = qg_ref.shape[0]
    kv_rank = kvg_ref.shape[0]
    qct = _rmsnorm_rows(at[:q_rank], qg_ref[...]).astype(BF16)
    ckvt = _rmsnorm_rows(at[q_rank:q_rank + kv_rank], kvg_ref[...])
    kpet = at[q_rank + kv_rank:]
    if emit_latent:
        refs[3][...] = ckvt.T
        refs[4][...] = kpet.T
    qt = _dot(wuq_ref[...], qct)
    qc, qa, qb = _rope_gains(qng_ref[...] * (QK_DIM ** -0.5 * math.log2(math.e)), rope)
    for hd in range(N_HEADS):
        sl = slice(hd * HEAD_SLOT, (hd + 1) * HEAD_SLOT)
        x = qt[sl]
        q_ref[sl, :] = _slab_norm_rope(x, qc, _partner_term(x, qa, qb) if use_rope else None).astype(BF16)
    kw = N_HEADS * HEAD_SLOT
    kvt = _dot(wukv_ref[...], ckvt.astype(BF16))
    v_ref[...] = kvt[kw:].astype(BF16)
    kc, ka, kb = _rope_gains(kng_ref[...], rope)
    partner = _partner_term(kpet, ka, kb) if use_rope else None
    for hd in range(N_HEADS):
        sl = slice(hd * HEAD_SLOT, (hd + 1) * HEAD_SLOT)
        k_ref[:, sl] = _slab_norm_rope(kvt[sl] + kpet, kc, partner).T.astype(BF16)


def _mla_prep(x, mod, g, wcat_t, qg, kvg, wuq_t, wukv_t, qng3, kng3, rope_t, *, row_base, rows_per_batch,
              emit_latent):
    n, d = x.shape
    seq = n if rows_per_batch is None else rows_per_batch
    tm = min(512, seq)
    tpb = None if rows_per_batch is None else rows_per_batch // tm
    kw = N_HEADS * HEAD_SLOT
    vw = N_HEADS * V_DIM
    q_rank, kv_rank = qg.shape[0], kvg.shape[0]
    full = lambda a: pl.BlockSpec(a.shape, lambda i: (0,) * a.ndim)
    args = [x, mod, g.reshape(1, d), wcat_t, qg.reshape(-1, 1), kvg.reshape(-1, 1), wuq_t, wukv_t, qng3, kng3]
    in_specs = [pl.BlockSpec((tm, d), lambda i: (i, 0))] + [full(a) for a in args[1:]]
    if rope_t is not None:
        pt = rows_per_batch // tm
        args += list(rope_t)
        in_specs += [pl.BlockSpec((HEAD_SLOT, tm), lambda i: (0, i % pt))] * 3
    row = lambda w: pl.BlockSpec((tm, w), lambda i: (i, 0))
    col = lambda w: pl.BlockSpec((w, tm), lambda i: (0, i))
    out_specs = [col(kw), row(kw), col(vw)]
    out_shape = [jax.ShapeDtypeStruct((kw, n), BF16), jax.ShapeDtypeStruct((n, kw), BF16),
                 jax.ShapeDtypeStruct((vw, n), BF16)]
    if emit_latent:
        out_specs += [row(kv_rank), row(HEAD_SLOT)]
        out_shape += [jax.ShapeDtypeStruct((n, kv_rank), F32), jax.ShapeDtypeStruct((n, HEAD_SLOT), F32)]
    return pl.pallas_call(
        functools.partial(_mla_prep_kernel, row_base=row_base, tpb=tpb, use_rope=rope_t is not None,
                          emit_latent=emit_latent),
        grid=(n // tm,),
        in_specs=in_specs, out_specs=out_specs, out_shape=out_shape,
        compiler_params=_cparams(1),
        name="mla_prep",
    )(*args)


def _mla_cache_kernel(ckv_ref, kpe_ref, wukv_ref, kng_ref, k_ref, v_ref):
    _expand_keys(ckv_ref[...].astype(BF16), kpe_ref[...], wukv_ref, kng_ref[...], k_ref, v_ref)


def _mla_cache(ckv, kpe_slot, wukv, kng):
    n = ckv.shape[0]
    tm = min(512, n)
    kw = N_HEADS * HEAD_SLOT
    vw = N_HEADS * V_DIM
    row = lambda w: pl.BlockSpec((tm, w), lambda i: (i, 0))
    return pl.pallas_call(
        _mla_cache_kernel,
        grid=(n // tm,),
        in_specs=[row(ckv.shape[1]), row(HEAD_SLOT), pl.BlockSpec(wukv.shape, lambda i: (0, 0)),
                  pl.BlockSpec(kng.shape, lambda i: (0, 0))],
        out_specs=[row(kw), pl.BlockSpec((vw, tm), lambda i: (0, i))],
        out_shape=[jax.ShapeDtypeStruct((n, kw), BF16), jax.ShapeDtypeStruct((vw, n), BF16)],
        compiler_params=_cparams(1),
        name="mla_cache_keys",
    )(ckv, kpe_slot, wukv, kng)


KEY_CHUNK = 256


def _attn_kernel(*refs, has_cache, heads):
    if has_cache:
        q_ref, k_ref, v_ref, kc_ref, vc_ref, o_ref = refs
    else:
        q_ref, k_ref, v_ref, o_ref = refs
    n = k_ref.shape[0]
    outs = []
    for hh in range(heads):
        sl = slice(hh * HEAD_SLOT, (hh + 1) * HEAD_SLOT)
        vs = slice(hh * V_DIM, (hh + 1) * V_DIM)
        q = q_ref[sl, :]
        pieces = [(k_ref, v_ref, c, min(KEY_CHUNK, n - c)) for c in range(0, n, KEY_CHUNK)]
        if has_cache:
            pieces.append((kc_ref, vc_ref, 0, kc_ref.shape[0]))
        sts = [_dot(kr[c:c + w, sl], q) for kr, _, c, w in pieces]
        mx = functools.reduce(jnp.maximum, [jnp.max(st, axis=0, keepdims=True) for st in sts])
        pts = [jnp.exp2(st - mx) for st in sts]
        l = functools.reduce(jnp.add, [jnp.sum(pt, axis=0, keepdims=True) for pt in pts])
        ot = functools.reduce(jnp.add, [_dot(vr[vs, c:c + w], pt.astype(BF16))
                                        for (_, vr, c, w), pt in zip(pieces, pts)])
        outs.append(ot / l)
    o_ref[...] = jnp.concatenate(outs, axis=0).T.astype(o_ref.dtype)


def _attention(q, k, v, kc, vc, *, batch, n, past):
    has_cache = kc is not None
    tq = min(512, n)
    heads = 2 if n > tq else N_HEADS
    nq = n // tq
    groups = N_HEADS // heads
    in_specs = [pl.BlockSpec((heads * HEAD_SLOT, tq), lambda b, j, i: (j, b * nq + i)),
                pl.BlockSpec((n, heads * HEAD_SLOT), lambda b, j, i: (b, j)),
                pl.BlockSpec((heads * V_DIM, n), lambda b, j, i: (j, b))]
    args = [q, k, v]
    if has_cache:
        in_specs += [pl.BlockSpec((past, heads * HEAD_SLOT), lambda b, j, i: (b, j)),
                     pl.BlockSpec((heads * V_DIM, past), lambda b, j, i: (j, b))]
        args += [kc, vc]
    return pl.pallas_call(
        functools.partial(_attn_kernel, has_cache=has_cache, heads=heads),
        grid=(batch, groups, nq),
        in_specs=in_specs,
        out_specs=pl.BlockSpec((tq, heads * V_DIM), lambda b, j, i: (b * nq + i, j)),
        out_shape=jax.ShapeDtypeStruct((batch * n, N_HEADS * V_DIM), BF16),
        compiler_params=_cparams(3),
        name="attention",
    )(*args)


def _pad_heads(w, width):
    k = w.shape[0]
    w = w.reshape(k, N_HEADS, width)
    return jnp.pad(w, ((0, 0), (0, 0), (0, HEAD_SLOT - width))).reshape(k, N_HEADS * HEAD_SLOT)


def _slot_gain(g):
    return jnp.pad(g, (0, HEAD_SLOT - QK_DIM)).reshape(1, HEAD_SLOT)


def _gain_columns(g_row):
    g = g_row.reshape(HEAD_SLOT)
    half = ROPE_DIM // 4
    return jnp.stack([g, jnp.roll(g, -half), jnp.roll(g, half)], axis=1)


def kernel(x_prompt, x_sample, cache_ckv, cache_kpe, c, c_ctx, norm1_g, norm2_g, mod_w, mod_b, mlp_w1, mlp_w2, sc_w_in, sc_conv_w, sc_conv_b, sc_w_out, mla_w_dq, mla_q_norm_g, mla_w_uq, mla_w_dkv, mla_kv_norm_g, mla_w_ukv, mla_qn_g, mla_kn_g, mla_w_o, hy_w_in, hy_conv_w, hy_conv_b, hy_f_w0, hy_f_b0, hy_f_w1, hy_f_b1, hy_f_w2, hy_f_b2, hy_f_w3, hy_sin_freq, hy_bias, hy_w_out):
    batch, seq, d = x_prompt.shape
    dec_batch, dec_seq, _ = x_sample.shape
    depth = mod_w.shape[0]
    n_mla = mla_w_dq.shape[0]
    past = cache_ckv.shape[2]
    kv_rank = mla_kv_norm_g.shape[1]
    assert 1 + dec_batch <= MOD_ROWS

    cond = jnp.zeros((MOD_ROWS, d), F32).at[0].set(c_ctx).at[1:1 + dec_batch].set(c)
    mods = _modulation(cond, mod_w, mod_b)

    w1 = mlp_w1.astype(BF16)
    w2 = mlp_w2.astype(BF16)
    sc_in = sc_w_in.astype(BF16)
    sc_out = sc_w_out.astype(BF16)
    hy_in = hy_w_in.astype(BF16)
    hy_out = hy_w_out.astype(BF16)
    mla_o = mla_w_o.astype(BF16)

    streams = [
        dict(x=x_prompt.reshape(batch * seq, d), row_base=0, rpb=None, seq=seq, nseq=batch, ctx=True),
        dict(x=x_sample.reshape(dec_batch * dec_seq, d), row_base=1, rpb=dec_seq, seq=dec_seq,
             nseq=dec_batch, ctx=False),
    ]
    hy_consts = {}
    new_ckv, new_kpe = [], []

    for i in range(depth):
        kind, j = i % 3, i // 3
        mod = mods[i]
        for st in streams:
            x = st["x"]
            rb, rpb, n = st["row_base"], st["rpb"], st["seq"]
            tail = None
            if kind == 0:
                x = _sconv(x, mod, norm1_g[i], sc_in, sc_conv_w[j], sc_conv_b[j], sc_out, j, seq_len=n,
                           row_base=rb, rows_per_batch=rpb)
            elif kind == 1:
                pe = mla_w_dkv[j][:, kv_rank:]
                pe_slot = jnp.pad(pe, ((0, 0), (QK_NOPE, HEAD_SLOT - QK_DIM)))
                wcat = jnp.concatenate([mla_w_dq[j], mla_w_dkv[j][:, :kv_rank], pe_slot], axis=1).astype(BF16)
                wuq = _pad_heads(mla_w_uq[j], QK_DIM).astype(BF16)
                ukv = mla_w_ukv[j].reshape(kv_rank, N_HEADS, QK_NOPE + V_DIM)
                wukv = jnp.concatenate(
                    [_pad_heads(ukv[:, :, :QK_NOPE].reshape(kv_rank, -1), QK_NOPE),
                     ukv[:, :, QK_NOPE:].reshape(kv_rank, -1)], axis=1).astype(BF16)
                qng, kng = _slot_gain(mla_qn_g[j]), _slot_gain(mla_kn_g[j])
                rope = None if st["ctx"] else _rope_tables(n)
                outs = _mla_prep(x, mod, norm1_g[i], wcat.T, mla_q_norm_g[j], mla_kv_norm_g[j], wuq.T, wukv.T,
                                 _gain_columns(qng), _gain_columns(kng), rope, row_base=rb,
                                 rows_per_batch=rpb, emit_latent=st["ctx"])
                q, k, v = outs[:3]
                if st["ctx"]:
                    new_ckv.append(outs[3].reshape(batch, seq, kv_rank))
                    new_kpe.append(outs[4][:, QK_NOPE:QK_DIM].reshape(batch, seq, ROPE_DIM))
                    kc = vc = None
                else:
                    ck = cache_ckv[:, j].reshape(dec_batch * past, kv_rank)
                    cp = jnp.pad(cache_kpe[:, j].reshape(dec_batch * past, ROPE_DIM),
                                 ((0, 0), (QK_NOPE, HEAD_SLOT - QK_DIM)))
                    kc, vc = _mla_cache(ck, cp, wukv, kng)
                o = _attention(q, k, v, kc, vc, batch=st["nseq"], n=n, past=past)
                tail = (o, mla_o, j)
            else:
                m = min(HY_BLOCK, n)
                if m not in hy_consts:
                    hy_consts[m] = _dft_constants(m)
                mats, minv = hy_consts[m]
                g = _hy_filter(n, m, hy_f_w0[j], hy_f_b0[j], hy_f_w1[j], hy_f_b1[j], hy_f_w2[j],
                               hy_f_b2[j], hy_f_w3[j], hy_sin_freq[j], mats)
                p = _nm_matmul(x, mod, norm1_g[i], hy_in, j, row_base=rb, rows_per_batch=rpb)
                z = _hy_conv(p, hy_conv_w[j], hy_conv_b[j], hy_bias[j], g, mats[FWD0], minv, n=n, m=m,
                             batch=st["nseq"])
                tail = (z, hy_out, j)
            x = _mlp(x, mod, norm2_g[i], w1, w2, i, row_base=rb, rows_per_batch=rpb, tail=tail)
            st["x"] = x

    y_prompt = streams[0]["x"].reshape(batch, seq, d)
    y_sample = streams[1]["x"].reshape(dec_batch, dec_seq, d)
    return (y_prompt, y_sample, jnp.stack(new_ckv, axis=1), jnp.stack(new_kpe, axis=1))
```

```python
import functools
import math

import numpy as np
import jax
import jax.numpy as jnp
from jax import lax
from jax.experimental import pallas as pl
from jax.experimental.pallas import tpu as pltpu

F32 = jnp.float32
BF16 = jnp.bfloat16

EPS = 1e-6
MOD_CHUNKS = 6
N_HEADS = 16
QK_NOPE = 64
ROPE_DIM = 32
QK_DIM = QK_NOPE + ROPE_DIM
V_DIM = 64
HEAD_SLOT = 128
GRID_W = 64
ROPE_THETA = 10000.0
HY_TARGET = 1e-2
HY_FAST = 0.3
HY_SLOW = 1.5
HY_BLOCK = 1024
NYQ_ROWS = 16
MOD_ROWS = 8

V7X_VMEM_BYTES = 64 * 1024 * 1024
VMEM_LIMIT = V7X_VMEM_BYTES - 8 * 1024 * 1024

SH1, SC1, G1, SH2, SC2, G2 = range(6)


def _tile(n, pref, align=128):
    if n <= pref:
        return n
    t = pref - pref % align
    while n % t:
        t -= align
    return t


def _cparams(n_axes):
    return pltpu.CompilerParams(dimension_semantics=("arbitrary",) * n_axes,
                                vmem_limit_bytes=VMEM_LIMIT)


def _mod_chunk(mod_ref, r, k, d):
    return mod_ref[pl.ds(r, 1), k * d:(k + 1) * d]


def _mod_row(i, row_base, tiles_per_batch):
    if tiles_per_batch is None:
        return row_base
    return row_base + i // tiles_per_batch


def _modnorm(x, g, sc, sh):
    ms = jnp.mean(x * x, axis=-1, keepdims=True)
    return x * lax.rsqrt(ms + EPS) * (g * (1.0 + sc)) + sh


def _dot(a, b):
    return jnp.dot(a, b, preferred_element_type=F32)


def _mod_kernel(c_ref, w_ref, b_ref, o_ref):
    c = c_ref[...]
    s = (c * jax.nn.sigmoid(c)).astype(BF16)
    o_ref[0] = _dot(s, w_ref[0].astype(BF16)) + b_ref[0]


def _modulation(cond, mod_w, mod_b):
    depth, d, n6 = mod_w.shape
    tn = _tile(n6, 1536)
    return pl.pallas_call(
        _mod_kernel,
        grid=(depth, n6 // tn),
        in_specs=[pl.BlockSpec((MOD_ROWS, d), lambda l, j: (0, 0)),
                  pl.BlockSpec((1, d, tn), lambda l, j: (l, 0, j)),
                  pl.BlockSpec((1, 1, tn), lambda l, j: (l, 0, j))],
        out_specs=pl.BlockSpec((1, MOD_ROWS, tn), lambda l, j: (l, 0, j)),
        out_shape=jax.ShapeDtypeStruct((depth, MOD_ROWS, n6), F32),
        compiler_params=_cparams(2),
        name="modulation",
    )(cond, mod_w, mod_b.reshape(depth, 1, n6))


def _resident(shape, index_map):
    return pl.BlockSpec(shape, index_map, pipeline_mode=pl.Buffered(1))


def _mlp_kernel(*refs, row_base, tpb, tf, mixer_tail):
    if mixer_tail:
        z_ref, wo_ref, refs = refs[0], refs[1], refs[2:]
    x_ref, mod_ref, g_ref, w1_ref, w2_ref, o_ref = refs
    d = x_ref.shape[1]
    r = _mod_row(pl.program_id(0), row_base, tpb)
    x = x_ref[...]
    if mixer_tail:
        x = x + _mod_chunk(mod_ref, r, G1, d) * _dot(z_ref[...], wo_ref[...])
    h = _modnorm(x, g_ref[...], _mod_chunk(mod_ref, r, SC2, d), _mod_chunk(mod_ref, r, SH2, d)).astype(BF16)
    acc = None
    for c in range(0, w1_ref.shape[1], tf):
        a = jnp.maximum(_dot(h, w1_ref[:, c:c + tf]), 0.0)
        y = _dot((a * a).astype(BF16), w2_ref[c:c + tf, :])
        acc = y if acc is None else acc + y
    o_ref[...] = x + _mod_chunk(mod_ref, r, G2, d) * acc


def _mlp(x, mod, g, w1, w2, layer, *, row_base, rows_per_batch, tail=None):
    n, d = x.shape
    dff = w1.shape[2]
    tm = min(1024, n if rows_per_batch is None else rows_per_batch)
    tpb = None if rows_per_batch is None else rows_per_batch // tm
    args = [x, mod, g.reshape(1, d), w1, w2]
    in_specs = [pl.BlockSpec((tm, d), lambda i: (i, 0)),
                pl.BlockSpec(mod.shape, lambda i: (0, 0)),
                pl.BlockSpec((1, d), lambda i: (0, 0)),
                _resident((None, d, dff), lambda i: (layer, 0, 0)),
                _resident((None, dff, d), lambda i: (layer, 0, 0))]
    if tail is not None:
        z, w_out, tail_layer = tail
        k = z.shape[1]
        args = [z, w_out] + args
        in_specs = [pl.BlockSpec((tm, k), lambda i: (i, 0)),
                    _resident((None, k, d), lambda i: (tail_layer, 0, 0))] + in_specs
    return pl.pallas_call(
        functools.partial(_mlp_kernel, row_base=row_base, tpb=tpb, tf=_tile(dff, 1024),
                          mixer_tail=tail is not None),
        grid=(n // tm,),
        in_specs=in_specs,
        out_specs=pl.BlockSpec((tm, d), lambda i: (i, 0)),
        out_shape=jax.ShapeDtypeStruct((n, d), F32),
        compiler_params=_cparams(1),
        name="mlp",
    )(*args)


HALO = 16


def _sconv_kernel(x_ref, xp_ref, xn_ref, mod_ref, g_ref, win_ref, cw_ref, cb_ref, wout_ref, o_ref, *,
                  seq_len, row_base, tpb):
    i = pl.program_id(0)
    tm, d = x_ref.shape
    r = _mod_row(i, row_base, tpb)
    x = x_ref[...]
    xcat = jnp.concatenate([xp_ref[...], x, xn_ref[...]], axis=0)
    h = _modnorm(xcat, g_ref[...], _mod_chunk(mod_ref, r, SC1, d), _mod_chunk(mod_ref, r, SH1, d)).astype(BF16)
    cx = _dot(h, win_ref[:, d:2 * d]) * _dot(h, win_ref[:, 2 * d:3 * d])
    rows = tm + 2 * HALO
    inner = slice(HALO, HALO + tm)
    row = lax.broadcasted_iota(jnp.int32, (tm, 1), 0)
    pos = jnp.bitwise_and(i * tm + row, seq_len - 1)
    dn = jnp.where(pos == 0, 0.0, pltpu.roll(cx, 1, 0)[inner])
    up = jnp.where(pos == seq_len - 1, 0.0, pltpu.roll(cx, rows - 1, 0)[inner])
    cw = cw_ref[...]
    u = dn * cw[0:1, :] + cx[inner] * cw[1:2, :] + up * cw[2:3, :] + cb_ref[...]
    z = (_dot(h[inner], win_ref[:, 0:d]) * u).astype(BF16)
    o_ref[...] = x + _mod_chunk(mod_ref, r, G1, d) * _dot(z, wout_ref[...])


def _sconv(x, mod, g, w_in, conv_w, conv_b, w_out, layer, *, seq_len, row_base, rows_per_batch):
    n, d = x.shape
    assert seq_len & (seq_len - 1) == 0
    tm = min(1024, n if rows_per_batch is None else rows_per_batch)
    tpb = None if rows_per_batch is None else rows_per_batch // tm
    hb = tm // HALO
    last = n // HALO - 1
    return pl.pallas_call(
        functools.partial(_sconv_kernel, seq_len=seq_len, row_base=row_base, tpb=tpb),
        grid=(n // tm,),
        in_specs=[pl.BlockSpec((tm, d), lambda i: (i, 0)),
                  pl.BlockSpec((HALO, d), lambda i: (jnp.maximum(i * hb - 1, 0), 0)),
                  pl.BlockSpec((HALO, d), lambda i: (jnp.minimum((i + 1) * hb, last), 0)),
                  pl.BlockSpec(mod.shape, lambda i: (0, 0)),
                  pl.BlockSpec((1, d), lambda i: (0, 0)),
                  _resident((None, d, 3 * d), lambda i: (layer, 0, 0)),
                  pl.BlockSpec((3, d), lambda i: (0, 0)),
                  pl.BlockSpec((1, d), lambda i: (0, 0)),
                  _resident((None, d, d), lambda i: (layer, 0, 0))],
        out_specs=pl.BlockSpec((tm, d), lambda i: (i, 0)),
        out_shape=jax.ShapeDtypeStruct((n, d), F32),
        compiler_params=_cparams(1),
        name="sconv",
    )(x, x, x, mod, g.reshape(1, d), w_in, conv_w, conv_b.reshape(1, d), w_out)


def _dft_constants(m):
    k = np.arange(m)[:, None]
    s = np.arange(m)[None, :]
    ang = np.pi * ((k * s) % (2 * m)) / m
    rows = 2 * m + NYQ_ROWS
    alt_k = (-1.0) ** np.arange(m)[:, None]
    alt_s = (-1.0) ** np.arange(m)

    def mat(re, im, drop_first):
        out = np.zeros((rows, m))
        out[:m], out[m:2 * m], out[2 * m] = re, im, alt_s
        if drop_first:
            out[:, 0] = 0.0
        return out

    cos, sin = np.cos(ang), np.sin(ang)
    mats = np.stack([mat(cos, -sin, False), mat(alt_k * cos, -alt_k * sin, True),
                     mat(cos, sin, False), mat(alt_k * cos, alt_k * sin, True)])
    ck = np.full((1, m), 2.0)
    ck[0, 0] = 1.0
    minv = np.concatenate([ck * cos.T, -ck * sin.T], axis=1) / (2 * m)
    return jnp.asarray(mats, F32).astype(BF16), jnp.asarray(minv, F32).astype(BF16)


FWD0, FWD1, BWD0, BWD1 = range(4)


def _filter_features(n, n_emb_pad):
    bands = (33 - 1) // 2
    t01 = np.linspace(0.0, 1.0, n)[:, None]
    w = 2.0 * np.pi * np.arange(n)[:, None] / n
    fb = np.linspace(1e-4, bands - 1, bands)[None, :]
    z = np.concatenate([t01, np.cos(fb * w), -np.sin(fb * w)], axis=-1)
    out = np.zeros((n, n_emb_pad))
    out[:, :z.shape[1]] = z
    return jnp.asarray(out, F32)


def _hy_filter_kernel(z_ref, w0_ref, b0_ref, w1_ref, b1_ref, w2_ref, b2_ref, fr_ref, w3b_ref, w3f_ref,
                      dl_ref, mats_ref, g_ref, a3_ref, f_ref, *, n, m):
    c = pl.program_id(0)
    dstep = pl.program_id(1)
    nb = n // m

    @pl.when(jnp.logical_and(c == 0, dstep == 0))
    def _():
        fr = fr_ref[...]
        a = jnp.sin(fr * (_dot(z_ref[...].astype(BF16), w0_ref[...]) + b0_ref[...]))
        a = jnp.sin(fr * (_dot(a.astype(BF16), w1_ref[...]) + b1_ref[...]))
        a = jnp.sin(fr * (_dot(a.astype(BF16), w2_ref[...]) + b2_ref[...]))
        a3_ref[...] = a.astype(BF16)

    @pl.when(dstep == 0)
    def _():
        t = lax.broadcasted_iota(jnp.int32, (n, 1), 0)
        decay = jnp.exp(-(t.astype(F32) / (n - 1.0)) * dl_ref[...])
        a3 = a3_ref[...]
        hb = jnp.where(t == 0, 0.0, _dot(a3, w3b_ref[...]) * decay)
        hf = _dot(a3, w3f_ref[...]) * decay
        norm = (jnp.sum(jnp.abs(hb), axis=0, keepdims=True)
                + jnp.sum(jnp.abs(hf), axis=0, keepdims=True))
        f_ref[0:n, :] = (hb / norm).astype(BF16)
        f_ref[n:2 * n, :] = (hf / norm).astype(BF16)

    d = dstep - (nb - 1)
    ia = jnp.where(d >= 0, FWD0, BWD0)
    ra = jnp.where(d >= 0, n + d * m, -d * m)
    ib = jnp.where(d >= 1, FWD1, jnp.where(d == 0, BWD0, BWD1))
    rb = jnp.where(d >= 1, n + (d - 1) * m, jnp.where(d == 0, 0, (-d - 1) * m))
    ra = pl.multiple_of(ra, m)
    rb = pl.multiple_of(rb, m)
    g = _dot(mats_ref[ia], f_ref[pl.ds(ra, m), :]) + _dot(mats_ref[ib], f_ref[pl.ds(rb, m), :])
    g_ref[0] = g.astype(g_ref.dtype)


def _hy_filter(n, m, w0, b0, w1, b1, w2, b2, w3, freq, mats):
    d = w3.shape[1] // 2
    order = w1.shape[0]
    nb = n // m
    ct = _tile(d, 256)
    emb_pad = 128
    z = _filter_features(n, emb_pad)
    w0p = jnp.zeros((emb_pad, order), F32).at[:w0.shape[0]].set(w0).astype(BF16)
    lo, hi = math.log(HY_TARGET) / HY_SLOW, math.log(HY_TARGET) / HY_FAST
    deltas = jnp.asarray(np.abs(np.linspace(lo, hi, d))[None, :], F32)
    w3b = w3.astype(BF16)
    rows = 2 * m + NYQ_ROWS
    full = lambda shp: pl.BlockSpec(shp, lambda c, s: (0,) * len(shp))
    return pl.pallas_call(
        functools.partial(_hy_filter_kernel, n=n, m=m),
        grid=(d // ct, 2 * nb - 1),
        in_specs=[full((n, emb_pad)), full((emb_pad, order)), full((1, order)),
                  full((order, order)), full((1, order)), full((order, order)), full((1, order)),
                  full((1, order)),
                  pl.BlockSpec((order, ct), lambda c, s: (0, d // ct + c)),
                  pl.BlockSpec((order, ct), lambda c, s: (0, c)),
                  pl.BlockSpec((1, ct), lambda c, s: (0, c)),
                  _resident((4, rows, m), lambda c, s: (0, 0, 0))],
        out_specs=pl.BlockSpec((1, rows, ct), lambda c, s: (s, 0, c)),
        out_shape=jax.ShapeDtypeStruct((2 * nb - 1, rows, d), BF16),
        scratch_shapes=[pltpu.VMEM((n, order), BF16), pltpu.VMEM((2 * n, ct), BF16)],
        compiler_params=_cparams(2),
        name="hyena_filter",
    )(z, w0p, b0.reshape(1, order), w1.astype(BF16), b1.reshape(1, order), w2.astype(BF16),
      b2.reshape(1, order), freq.reshape(1, order), w3b, w3b, deltas, mats)


def _hyena_in_kernel(x_ref, xp_ref, xn_ref, mod_ref, g_ref, win_ref, cw_ref, cb_ref, x0_ref, v_ref, *,
                     seq_len, row_base, tpb, ct):
    i = pl.program_id(0)
    tm, d = x_ref.shape
    r = _mod_row(i, row_base, tpb)
    xcat = jnp.concatenate([xp_ref[...], x_ref[...], xn_ref[...]], axis=0)
    h = _modnorm(xcat, g_ref[...], _mod_chunk(mod_ref, r, SC1, d), _mod_chunk(mod_ref, r, SH1, d)).astype(BF16)
    rows = tm + 2 * HALO
    inner = slice(HALO, HALO + tm)
    row = lax.broadcasted_iota(jnp.int32, (tm, 1), 0)
    pos = jnp.bitwise_and(i * tm + row, seq_len - 1)
    first, last = pos == 0, pos == seq_len - 1

    def conv(col):
        p = _dot(h, win_ref[:, col:col + ct])
        w = cw_ref[:, col:col + ct]
        dn = jnp.where(first, 0.0, pltpu.roll(p, 1, 0)[inner])
        up = jnp.where(last, 0.0, pltpu.roll(p, rows - 1, 0)[inner])
        return dn * w[0:1, :] + p[inner] * w[1:2, :] + up * w[2:3, :] + cb_ref[:, col:col + ct]

    for c in range(0, d, ct):
        x0_ref[:, c:c + ct] = conv(c).astype(BF16)
        v_ref[:, c:c + ct] = (conv(2 * d + c) * conv(d + c)).astype(BF16)


def _hyena_in(x, mod, g, w_in, layer, conv_w, conv_b, *, seq_len, row_base, rows_per_batch):
    n, d = x.shape
    assert seq_len & (seq_len - 1) == 0
    tm = min(1024, n if rows_per_batch is None else rows_per_batch)
    tpb = None if rows_per_batch is None else rows_per_batch // tm
    hb = tm // HALO
    last = n // HALO - 1
    out = pl.BlockSpec((tm, d), lambda i: (i, 0))
    return pl.pallas_call(
        functools.partial(_hyena_in_kernel, seq_len=seq_len, row_base=row_base, tpb=tpb, ct=_tile(d, 256)),
        grid=(n // tm,),
        in_specs=[pl.BlockSpec((tm, d), lambda i: (i, 0)),
                  pl.BlockSpec((HALO, d), lambda i: (jnp.maximum(i * hb - 1, 0), 0)),
                  pl.BlockSpec((HALO, d), lambda i: (jnp.minimum((i + 1) * hb, last), 0)),
                  pl.BlockSpec(mod.shape, lambda i: (0, 0)),
                  pl.BlockSpec((1, d), lambda i: (0, 0)),
                  _resident((None, d, 3 * d), lambda i: (layer, 0, 0)),
                  pl.BlockSpec((3, 3 * d), lambda i: (0, 0)),
                  pl.BlockSpec((1, 3 * d), lambda i: (0, 0))],
        out_specs=[out, out],
        out_shape=[jax.ShapeDtypeStruct((n, d), BF16), jax.ShapeDtypeStruct((n, d), BF16)],
        compiler_params=_cparams(1),
        name="hyena_in",
    )(x, x, x, mod, g.reshape(1, d), w_in, conv_w, conv_b.reshape(1, 3 * d))


def _hy_conv_kernel(x0_ref, v_ref, hb_ref, g_ref, ma_ref, minv_ref, o_ref, vf_ref, *, m, nb):
    i = pl.program_id(2)

    @pl.when(i == 0)
    def _():
        for j in range(nb):
            vf_ref[j] = _dot(ma_ref[...], v_ref[j * m:(j + 1) * m, :])

    ct = o_ref.shape[1]
    acc_r = jnp.zeros((m, ct), F32)
    acc_i = jnp.zeros((m, ct), F32)
    acc_n = jnp.zeros((NYQ_ROWS, ct), F32)
    for j in range(nb):
        g = g_ref[i - j + nb - 1].astype(F32)
        vr = vf_ref[j, 0:m, :]
        vi = vf_ref[j, m:2 * m, :]
        gr = g[0:m, :]
        gi = g[m:2 * m, :]
        acc_r += gr * vr - gi * vi
        acc_i += gr * vi + gi * vr
        acc_n += g[2 * m:, :] * vf_ref[j, 2 * m:, :]
    spec = jnp.concatenate([acc_r, acc_i], axis=0).astype(BF16)
    t = lax.broadcasted_iota(jnp.int32, (m, 1), 0)
    sgn = (1 - 2 * jnp.bitwise_and(t, 1)).astype(F32) * (0.5 / m)
    y = _dot(minv_ref[...], spec) + sgn * acc_n[0:1, :]
    start = pl.multiple_of(i * m, m)
    v1 = v_ref[pl.ds(start, m), :].astype(F32)
    x0 = x0_ref[pl.ds(start, m), :].astype(F32)
    o_ref[...] = ((y + v1 * hb_ref[...]) * x0).astype(o_ref.dtype)


def _hy_conv_short_kernel(x0_ref, v_ref, hb_ref, g_ref, ma_ref, minv_ref, o_ref, *, m, spb):
    g = g_ref[0].astype(F32)
    gr, gi, gn = g[0:m, :], g[m:2 * m, :], g[2 * m:, :]
    t = lax.broadcasted_iota(jnp.int32, (m, 1), 0)
    sgn = (1 - 2 * jnp.bitwise_and(t, 1)).astype(F32) * (0.5 / m)
    for s in range(spb):
        rows = slice(s * m, (s + 1) * m)
        vb = v_ref[rows, :]
        vf = _dot(ma_ref[...], vb)
        vr, vi = vf[0:m, :], vf[m:2 * m, :]
        spec = jnp.concatenate([gr * vr - gi * vi, gr * vi + gi * vr], axis=0).astype(BF16)
        y = _dot(minv_ref[...], spec) + sgn * (gn * vf[2 * m:, :])[0:1, :]
        o_ref[rows, :] = ((y + vb.astype(F32) * hb_ref[...]) * x0_ref[rows, :].astype(F32)).astype(o_ref.dtype)


def _hy_conv(x0, v, hy_bias, g, ma, minv, *, n, m, batch):
    d = hy_bias.shape[0]
    nb = n // m
    ct = _tile(d, 256)
    rows = 2 * m + NYQ_ROWS
    if nb == 1:
        spb = max(s for s in range(1, 9) if batch % s == 0)
        blk = pl.BlockSpec((spb * n, ct), lambda c, b: (b, c))
        return pl.pallas_call(
            functools.partial(_hy_conv_short_kernel, m=m, spb=spb),
            grid=(d // ct, batch // spb),
            in_specs=[blk, blk,
                      pl.BlockSpec((1, ct), lambda c, b: (0, c)),
                      pl.BlockSpec((1, rows, ct), lambda c, b: (0, 0, c)),
                      _resident((rows, m), lambda c, b: (0, 0)),
                      _resident((m, 2 * m), lambda c, b: (0, 0))],
            out_specs=blk,
            out_shape=jax.ShapeDtypeStruct((batch * n, d), BF16),
            compiler_params=_cparams(2),
            name="hyena_conv_short",
        )(x0, v, hy_bias.reshape(1, d), g, ma, minv)
    seq = pl.BlockSpec((n, ct), lambda c, b, i: (b, c))
    return pl.pallas_call(
        functools.partial(_hy_conv_kernel, m=m, nb=nb),
        grid=(d // ct, batch, nb),
        in_specs=[seq, seq,
                  pl.BlockSpec((1, ct), lambda c, b, i: (0, c)),
                  _resident((2 * nb - 1, rows, ct), lambda c, b, i: (0, 0, c)),
                  _resident((rows, m), lambda c, b, i: (0, 0)),
                  _resident((m, 2 * m), lambda c, b, i: (0, 0))],
        out_specs=pl.BlockSpec((m, ct), lambda c, b, i: (b * nb + i, c)),
        out_shape=jax.ShapeDtypeStruct((batch * n, d), BF16),
        scratch_shapes=[pltpu.VMEM((nb, rows, ct), F32)],
        compiler_params=_cparams(3),
        name="hyena_conv",
    )(x0, v, hy_bias.reshape(1, d), g, ma, minv)


def _rope_tables(n):
    axis = ROPE_DIM // 2
    half = axis // 2
    inv = ROPE_THETA ** (-np.arange(0, axis, 2) / axis)
    t = np.arange(n)
    row, col = t // GRID_W, t % GRID_W
    cos = np.ones((n, HEAD_SLOT))
    sin_a = np.zeros((n, HEAD_SLOT))
    sin_b = np.zeros((n, HEAD_SLOT))
    for k, pos in enumerate((row, col)):
        ang = pos[:, None] * inv[None, :]
        base = QK_NOPE + k * axis
        cos[:, base:base + half] = np.cos(ang)
        cos[:, base + half:base + axis] = np.cos(ang)
        sin_a[:, base:base + half] = -np.sin(ang)
        sin_b[:, base + half:base + axis] = np.sin(ang)
    return jnp.asarray(cos.T, F32), jnp.asarray(sin_a.T, F32), jnp.asarray(sin_b.T, F32)


def _expand_keys(ckv_b, kpe_slot, wukv_ref, kng, k_ref, v_ref):
    kw = N_HEADS * HEAD_SLOT
    kv = _dot(ckv_b, wukv_ref[...])
    for h in range(N_HEADS):
        sl = slice(h * HEAD_SLOT, (h + 1) * HEAD_SLOT)
        x = kv[:, sl] + kpe_slot
        scale = lax.rsqrt(jnp.sum(x * x, axis=-1, keepdims=True) * (1.0 / QK_DIM) + EPS)
        k_ref[:, sl] = (x * scale * kng).astype(BF16)
    v_ref[...] = kv[:, kw:].T.astype(BF16)


def _rmsnorm_rows(x, g_col):
    ms = jnp.mean(x * x, axis=0, keepdims=True)
    return x * lax.rsqrt(ms + EPS) * g_col


def _rope_gains(g3, rope):
    if rope is None:
        return g3[:, 0:1], None, None
    cos, sin_a, sin_b = rope
    return g3[:, 0:1] * cos, g3[:, 1:2] * sin_a, g3[:, 2:3] * sin_b


def _partner_term(x, ga, gb):
    half = ROPE_DIM // 4
    return pltpu.roll(x, HEAD_SLOT - half, 0) * ga + pltpu.roll(x, half, 0) * gb


def _slab_norm_rope(x, gc, partner):
    scale = lax.rsqrt(jnp.sum(x * x, axis=0, keepdims=True) * (1.0 / QK_DIM) + EPS)
    y = x * gc
    if partner is not None:
        y = y + partner
    return y * scale


def _mla_prep_kernel(*refs, row_base, tpb, use_rope, emit_latent):
    (x_ref, mod_ref, g_ref, wcat_ref, qg_ref, kvg_ref, wuq_ref, wukv_ref, qng_ref, kng_ref), refs = refs[:10], refs[10:]
    if use_rope:
        (cos_ref, sa_ref, sb_ref), refs = refs[:3], refs[3:]
        rope = (cos_ref[...], sa_ref[...], sb_ref[...])
    else:
        rope = None
    q_ref, k_ref, v_ref = refs[:3]
    i = pl.program_id(0)
    d = x_ref.shape[1]
    r = _mod_row(i, row_base, tpb)
    h = _modnorm(x_ref[...], g_ref[...], _mod_chunk(mod_ref, r, SC1, d), _mod_chunk(mod_ref, r, SH1, d))
    at = _dot(wcat_ref[...], h.T.astype(BF16))
    q_rank = qg_ref.shape[0]
    kv_rank = kvg_ref.shape[0]
    qct = _rmsnorm_rows(at[:q_rank], qg_ref[...]).astype(BF16)
    ckvt = _rmsnorm_rows(at[q_rank:q_rank + kv_rank], kvg_ref[...])
    kpet = at[q_rank + kv_rank:]
    if emit_latent:
        refs[3][...] = ckvt.T
        refs[4][...] = kpet.T
    qt = _dot(wuq_ref[...], qct)
    qc, qa, qb = _rope_gains(qng_ref[...] * (QK_DIM ** -0.5 * math.log2(math.e)), rope)
    for hd in range(N_HEADS):
        sl = slice(hd * HEAD_SLOT, (hd + 1) * HEAD_SLOT)
        x = qt[sl]
        q_ref[sl, :] = _slab_norm_rope(x, qc, _partner_term(x, qa, qb) if use_rope else None).astype(BF16)
    kw = N_HEADS * HEAD_SLOT
    kvt = _dot(wukv_ref[...], ckvt.astype(BF16))
    v_ref[...] = kvt[kw:].astype(BF16)
    kc, ka, kb = _rope_gains(kng_ref[...], rope)
    partner = _partner_term(kpet, ka, kb) if use_rope else None
    for hd in range(N_HEADS):
        sl = slice(hd * HEAD_SLOT, (hd + 1) * HEAD_SLOT)
        k_ref[:, sl] = _slab_norm_rope(kvt[sl] + kpet, kc, partner).T.astype(BF16)


def _mla_prep(x, mod, g, wcat_t, qg, kvg, wuq_t, wukv_t, qng3, kng3, rope_t, *, row_base, rows_per_batch,
              emit_latent):
    n, d = x.shape
    seq = n if rows_per_batch is None else rows_per_batch
    tm = min(512, seq)
    tpb = None if rows_per_batch is None else rows_per_batch // tm
    kw = N_HEADS * HEAD_SLOT
    vw = N_HEADS * V_DIM
    q_rank, kv_rank = qg.shape[0], kvg.shape[0]
    full = lambda a: pl.BlockSpec(a.shape, lambda i: (0,) * a.ndim)
    args = [x, mod, g.reshape(1, d), wcat_t, qg.reshape(-1, 1), kvg.reshape(-1, 1), wuq_t, wukv_t, qng3, kng3]
    in_specs = [pl.BlockSpec((tm, d), lambda i: (i, 0))] + [full(a) for a in args[1:]]
    if rope_t is not None:
        pt = rows_per_batch // tm
        args += list(rope_t)
        in_specs += [pl.BlockSpec((HEAD_SLOT, tm), lambda i: (0, i % pt))] * 3
    row = lambda w: pl.BlockSpec((tm, w), lambda i: (i, 0))
    col = lambda w: pl.BlockSpec((w, tm), lambda i: (0, i))
    out_specs = [col(kw), row(kw), col(vw)]
    out_shape = [jax.ShapeDtypeStruct((kw, n), BF16), jax.ShapeDtypeStruct((n, kw), BF16),
                 jax.ShapeDtypeStruct((vw, n), BF16)]
    if emit_latent:
        out_specs += [row(kv_rank), row(HEAD_SLOT)]
        out_shape += [jax.ShapeDtypeStruct((n, kv_rank), F32), jax.ShapeDtypeStruct((n, HEAD_SLOT), F32)]
    return pl.pallas_call(
        functools.partial(_mla_prep_kernel, row_base=row_base, tpb=tpb, use_rope=rope_t is not None,
                          emit_latent=emit_latent),
        grid=(n // tm,),
        in_specs=in_specs, out_specs=out_specs, out_shape=out_shape,
        compiler_params=_cparams(1),
        name="mla_prep",
    )(*args)


def _mla_cache_kernel(ckv_ref, kpe_ref, wukv_ref, kng_ref, k_ref, v_ref):
    _expand_keys(ckv_ref[...].astype(BF16), kpe_ref[...], wukv_ref, kng_ref[...], k_ref, v_ref)


def _mla_cache(ckv, kpe_slot, wukv, kng):
    n = ckv.shape[0]
    tm = min(512, n)
    kw = N_HEADS * HEAD_SLOT
    vw = N_HEADS * V_DIM
    row = lambda w: pl.BlockSpec((tm, w), lambda i: (i, 0))
    return pl.pallas_call(
        _mla_cache_kernel,
        grid=(n // tm,),
        in_specs=[row(ckv.shape[1]), row(HEAD_SLOT), pl.BlockSpec(wukv.shape, lambda i: (0, 0)),
                  pl.BlockSpec(kng.shape, lambda i: (0, 0))],
        out_specs=[row(kw), pl.BlockSpec((vw, tm), lambda i: (0, i))],
        out_shape=[jax.ShapeDtypeStruct((n, kw), BF16), jax.ShapeDtypeStruct((vw, n), BF16)],
        compiler_params=_cparams(1),
        name="mla_cache_keys",
    )(ckv, kpe_slot, wukv, kng)


KEY_CHUNK = 256


def _attn_kernel(*refs, has_cache, heads):
    if has_cache:
        q_ref, k_ref, v_ref, kc_ref, vc_ref, o_ref = refs
    else:
        q_ref, k_ref, v_ref, o_ref = refs
    n = k_ref.shape[0]
    outs = []
    for hh in range(heads):
        sl = slice(hh * HEAD_SLOT, (hh + 1) * HEAD_SLOT)
        vs = slice(hh * V_DIM, (hh + 1) * V_DIM)
        q = q_ref[sl, :]
        pieces = [(k_ref, v_ref, c, min(KEY_CHUNK, n - c)) for c in range(0, n, KEY_CHUNK)]
        if has_cache:
            pieces.append((kc_ref, vc_ref, 0, kc_ref.shape[0]))
        sts = [_dot(kr[c:c + w, sl], q) for kr, _, c, w in pieces]
        mx = functools.reduce(jnp.maximum, [jnp.max(st, axis=0, keepdims=True) for st in sts])
        pts = [jnp.exp2(st - mx) for st in sts]
        l = functools.reduce(jnp.add, [jnp.sum(pt, axis=0, keepdims=True) for pt in pts])
        ot = functools.reduce(jnp.add, [_dot(vr[vs, c:c + w], pt.astype(BF16))
                                        for (_, vr, c, w), pt in zip(pieces, pts)])
        outs.append(ot / l)
    o_ref[...] = jnp.concatenate(outs, axis=0).T.astype(o_ref.dtype)


def _attention(q, k, v, kc, vc, *, batch, n, past):
    has_cache = kc is not None
    tq = min(512, n)
    heads = 2 if n > tq else N_HEADS
    nq = n // tq
    groups = N_HEADS // heads
    in_specs = [pl.BlockSpec((heads * HEAD_SLOT, tq), lambda b, j, i: (j, b * nq + i)),
                pl.BlockSpec((n, heads * HEAD_SLOT), lambda b, j, i: (b, j)),
                pl.BlockSpec((heads * V_DIM, n), lambda b, j, i: (j, b))]
    args = [q, k, v]
    if has_cache:
        in_specs += [pl.BlockSpec((past, heads * HEAD_SLOT), lambda b, j, i: (b, j)),
                     pl.BlockSpec((heads * V_DIM, past), lambda b, j, i: (j, b))]
        args += [kc, vc]
    return pl.pallas_call(
        functools.partial(_attn_kernel, has_cache=has_cache, heads=heads),
        grid=(batch, groups, nq),
        in_specs=in_specs,
        out_specs=pl.BlockSpec((tq, heads * V_DIM), lambda b, j, i: (b * nq + i, j)),
        out_shape=jax.ShapeDtypeStruct((batch * n, N_HEADS * V_DIM), BF16),
        compiler_params=_cparams(3),
        name="attention",
    )(*args)


def _pad_heads(w, width):
    k = w.shape[0]
    w = w.reshape(k, N_HEADS, width)
    return jnp.pad(w, ((0, 0), (0, 0), (0, HEAD_SLOT - width))).reshape(k, N_HEADS * HEAD_SLOT)


def _slot_gain(g):
    return jnp.pad(g, (0, HEAD_SLOT - QK_DIM)).reshape(1, HEAD_SLOT)


def _gain_columns(g_row):
    g = g_row.reshape(HEAD_SLOT)
    half = ROPE_DIM // 4
    return jnp.stack([g, jnp.roll(g, -half), jnp.roll(g, half)], axis=1)


def kernel(x_prompt, x_sample, cache_ckv, cache_kpe, c, c_ctx, norm1_g, norm2_g, mod_w, mod_b, mlp_w1, mlp_w2, sc_w_in, sc_conv_w, sc_conv_b, sc_w_out, mla_w_dq, mla_q_norm_g, mla_w_uq, mla_w_dkv, mla_kv_norm_g, mla_w_ukv, mla_qn_g, mla_kn_g, mla_w_o, hy_w_in, hy_conv_w, hy_conv_b, hy_f_w0, hy_f_b0, hy_f_w1, hy_f_b1, hy_f_w2, hy_f_b2, hy_f_w3, hy_sin_freq, hy_bias, hy_w_out):
    batch, seq, d = x_prompt.shape
    dec_batch, dec_seq, _ = x_sample.shape
    depth = mod_w.shape[0]
    n_mla = mla_w_dq.shape[0]
    past = cache_ckv.shape[2]
    kv_rank = mla_kv_norm_g.shape[1]
    assert 1 + dec_batch <= MOD_ROWS

    cond = jnp.zeros((MOD_ROWS, d), F32).at[0].set(c_ctx).at[1:1 + dec_batch].set(c)
    mods = _modulation(cond, mod_w, mod_b)

    w1 = mlp_w1.astype(BF16)
    w2 = mlp_w2.astype(BF16)
    sc_in = sc_w_in.astype(BF16)
    sc_out = sc_w_out.astype(BF16)
    hy_in = hy_w_in.astype(BF16)
    hy_out = hy_w_out.astype(BF16)
    mla_o = mla_w_o.astype(BF16)

    streams = [
        dict(x=x_prompt.reshape(batch * seq, d), row_base=0, rpb=None, seq=seq, nseq=batch, ctx=True),
        dict(x=x_sample.reshape(dec_batch * dec_seq, d), row_base=1, rpb=dec_seq, seq=dec_seq,
             nseq=dec_batch, ctx=False),
    ]
    hy_consts = {}
    new_ckv, new_kpe = [], []

    for i in range(depth):
        kind, j = i % 3, i // 3
        mod = mods[i]
        for st in streams:
            x = st["x"]
            rb, rpb, n = st["row_base"], st["rpb"], st["seq"]
            tail = None
            if kind == 0:
                x = _sconv(x, mod, norm1_g[i], sc_in, sc_conv_w[j], sc_conv_b[j], sc_out, j, seq_len=n,
                           row_base=rb, rows_per_batch=rpb)
            elif kind == 1:
                pe = mla_w_dkv[j][:, kv_rank:]
                pe_slot = jnp.pad(pe, ((0, 0), (QK_NOPE, HEAD_SLOT - QK_DIM)))
                wcat = jnp.concatenate([mla_w_dq[j], mla_w_dkv[j][:, :kv_rank], pe_slot], axis=1).astype(BF16)
                wuq = _pad_heads(mla_w_uq[j], QK_DIM).astype(BF16)
                ukv = mla_w_ukv[j].reshape(kv_rank, N_HEADS, QK_NOPE + V_DIM)
                wukv = jnp.concatenate(
                    [_pad_heads(ukv[:, :, :QK_NOPE].reshape(kv_rank, -1), QK_NOPE),
                     ukv[:, :, QK_NOPE:].reshape(kv_rank, -1)], axis=1).astype(BF16)
                qng, kng = _slot_gain(mla_qn_g[j]), _slot_gain(mla_kn_g[j])
                rope = None if st["ctx"] else _rope_tables(n)
                outs = _mla_prep(x, mod, norm1_g[i], wcat.T, mla_q_norm_g[j], mla_kv_norm_g[j], wuq.T, wukv.T,
                                 _gain_columns(qng), _gain_columns(kng), rope, row_base=rb,
                                 rows_per_batch=rpb, emit_latent=st["ctx"])
                q, k, v = outs[:3]
                if st["ctx"]:
                    new_ckv.append(outs[3].reshape(batch, seq, kv_rank))
                    new_kpe.append(outs[4][:, QK_NOPE:QK_DIM].reshape(batch, seq, ROPE_DIM))
                    kc = vc = None
                else:
                    ck = cache_ckv[:, j].reshape(dec_batch * past, kv_rank)
                    cp = jnp.pad(cache_kpe[:, j].reshape(dec_batch * past, ROPE_DIM),
                                 ((0, 0), (QK_NOPE, HEAD_SLOT - QK_DIM)))
                    kc, vc = _mla_cache(ck, cp, wukv, kng)
                o = _attention(q, k, v, kc, vc, batch=st["nseq"], n=n, past=past)
                tail = (o, mla_o, j)
            else:
                m = min(HY_BLOCK, n)
                if m not in hy_consts:
                    hy_consts[m] = _dft_constants(m)
                mats, minv = hy_consts[m]
                g = _hy_filter(n, m, hy_f_w0[j], hy_f_b0[j], hy_f_w1[j], hy_f_b1[j], hy_f_w2[j],
                               hy_f_b2[j], hy_f_w3[j], hy_sin_freq[j], mats)
                x0, v1 = _hyena_in(x, mod, norm1_g[i], hy_in, j, hy_conv_w[j], hy_conv_b[j], seq_len=n,
                                   row_base=rb, rows_per_batch=rpb)
                z = _hy_conv(x0, v1, hy_bias[j], g, mats[FWD0], minv, n=n, m=m, batch=st["nseq"])
                tail = (z, hy_out, j)
            x = _mlp(x, mod, norm2_g[i], w1, w2, i, row_base=rb, rows_per_batch=rpb, tail=tail)
            st["x"] = x

    y_prompt = streams[0]["x"].reshape(batch, seq, d)
    y_sample = streams[1]["x"].reshape(dec_batch, dec_seq, d)
    return (y_prompt, y_sample, jnp.stack(new_ckv, axis=1), jnp.stack(new_kpe, axis=1))
```

```python
import functools
import math

import numpy as np
import jax
import jax.numpy as jnp
from jax import lax
from jax.experimental import pallas as pl
from jax.experimental.pallas import tpu as pltpu

F32 = jnp.float32
BF16 = jnp.bfloat16

EPS = 1e-6
MOD_CHUNKS = 6
N_HEADS = 16
QK_NOPE = 64
ROPE_DIM = 32
QK_DIM = QK_NOPE + ROPE_DIM
V_DIM = 64
HEAD_SLOT = 128
GRID_W = 64
ROPE_THETA = 10000.0
HY_TARGET = 1e-2
HY_FAST = 0.3
HY_SLOW = 1.5
HY_BLOCK = 1024
NYQ_ROWS = 16
MOD_ROWS = 8

V7X_VMEM_BYTES = 64 * 1024 * 1024
VMEM_LIMIT = V7X_VMEM_BYTES - 8 * 1024 * 1024

SH1, SC1, G1, SH2, SC2, G2 = range(6)


def _tile(n, pref, align=128):
    if n <= pref:
        return n
    t = pref - pref % align
    while n % t:
        t -= align
    return t


def _cparams(n_axes):
    return pltpu.CompilerParams(dimension_semantics=("arbitrary",) * n_axes,
                                vmem_limit_bytes=VMEM_LIMIT)


def _mod_chunk(mod_ref, r, k, d):
    return mod_ref[pl.ds(r, 1), k * d:(k + 1) * d]


def _mod_row(i, row_base, tiles_per_batch):
    if tiles_per_batch is None:
        return row_base
    return row_base + i // tiles_per_batch


def _modnorm(x, g, sc, sh):
    ms = jnp.mean(x * x, axis=-1, keepdims=True)
    return x * lax.rsqrt(ms + EPS) * (g * (1.0 + sc)) + sh


def _dot(a, b):
    return jnp.dot(a, b, preferred_element_type=F32)


def _mod_kernel(c_ref, w_ref, b_ref, o_ref):
    c = c_ref[...]
    s = (c * jax.nn.sigmoid(c)).astype(BF16)
    o_ref[0] = _dot(s, w_ref[0].astype(BF16)) + b_ref[0]


def _modulation(cond, mod_w, mod_b):
    depth, d, n6 = mod_w.shape
    tn = _tile(n6, 1536)
    return pl.pallas_call(
        _mod_kernel,
        grid=(depth, n6 // tn),
        in_specs=[pl.BlockSpec((MOD_ROWS, d), lambda l, j: (0, 0)),
                  pl.BlockSpec((1, d, tn), lambda l, j: (l, 0, j)),
                  pl.BlockSpec((1, 1, tn), lambda l, j: (l, 0, j))],
        out_specs=pl.BlockSpec((1, MOD_ROWS, tn), lambda l, j: (l, 0, j)),
        out_shape=jax.ShapeDtypeStruct((depth, MOD_ROWS, n6), F32),
        compiler_params=_cparams(2),
        name="modulation",
    )(cond, mod_w, mod_b.reshape(depth, 1, n6))


def _resident(shape, index_map):
    return pl.BlockSpec(shape, index_map, pipeline_mode=pl.Buffered(1))


def _mlp_kernel(*refs, row_base, tpb, tf, mixer_tail):
    if mixer_tail:
        z_ref, wo_ref, refs = refs[0], refs[1], refs[2:]
    x_ref, mod_ref, g_ref, w1_ref, w2_ref, o_ref = refs
    d = x_ref.shape[1]
    r = _mod_row(pl.program_id(0), row_base, tpb)
    x = x_ref[...]
    if mixer_tail:
        x = x + _mod_chunk(mod_ref, r, G1, d) * _dot(z_ref[...], wo_ref[...])
    h = _modnorm(x, g_ref[...], _mod_chunk(mod_ref, r, SC2, d), _mod_chunk(mod_ref, r, SH2, d)).astype(BF16)
    acc = None
    for c in range(0, w1_ref.shape[1], tf):
        a = jnp.maximum(_dot(h, w1_ref[:, c:c + tf]), 0.0)
        y = _dot((a * a).astype(BF16), w2_ref[c:c + tf, :])
        acc = y if acc is None else acc + y
    o_ref[...] = x + _mod_chunk(mod_ref, r, G2, d) * acc


def _mlp(x, mod, g, w1, w2, layer, *, row_base, rows_per_batch, tail=None):
    n, d = x.shape
    dff = w1.shape[2]
    tm = min(1024, n if rows_per_batch is None else rows_per_batch)
    tpb = None if rows_per_batch is None else rows_per_batch // tm
    args = [x, mod, g.reshape(1, d), w1, w2]
    in_specs = [pl.BlockSpec((tm, d), lambda i: (i, 0)),
                pl.BlockSpec(mod.shape, lambda i: (0, 0)),
                pl.BlockSpec((1, d), lambda i: (0, 0)),
                _resident((None, d, dff), lambda i: (layer, 0, 0)),
                _resident((None, dff, d), lambda i: (layer, 0, 0))]
    if tail is not None:
        z, w_out, tail_layer = tail
        k = z.shape[1]
        args = [z, w_out] + args
        in_specs = [pl.BlockSpec((tm, k), lambda i: (i, 0)),
                    _resident((None, k, d), lambda i: (tail_layer, 0, 0))] + in_specs
    return pl.pallas_call(
        functools.partial(_mlp_kernel, row_base=row_base, tpb=tpb, tf=_tile(dff, 1024),
                          mixer_tail=tail is not None),
        grid=(n // tm,),
        in_specs=in_specs,
        out_specs=pl.BlockSpec((tm, d), lambda i: (i, 0)),
        out_shape=jax.ShapeDtypeStruct((n, d), F32),
        compiler_params=_cparams(1),
        name="mlp",
    )(*args)


HALO = 16
SUBLANES = 8


def _seq_edges(i, tm, seq_len):
    if seq_len <= tm:
        return list(range(0, tm, seq_len)), list(range(seq_len - 1, tm, seq_len)), (None, None)
    mask = seq_len - 1
    return [0], [tm - 1], (jnp.bitwise_and(i * tm, mask) == 0, jnp.bitwise_and((i + 1) * tm, mask) == 0)


def _zero_rows(x, rows, live):
    sub = lax.broadcasted_iota(jnp.int32, (SUBLANES, 1), 0)
    pieces, done = [], 0
    for grp in sorted({r // SUBLANES for r in rows}):
        lo = grp * SUBLANES
        if lo > done:
            pieces.append(x[done:lo])
        hit = None
        for r in rows:
            if r // SUBLANES == grp:
                target = r % SUBLANES if live is None else jnp.where(live, r % SUBLANES, -1)
                hit = (sub == target) if hit is None else jnp.logical_or(hit, sub == target)
        pieces.append(jnp.where(hit, 0.0, x[lo:lo + SUBLANES]))
        done = lo + SUBLANES
    if done < x.shape[0]:
        pieces.append(x[done:])
    return jnp.concatenate(pieces, axis=0)


def _sconv_kernel(x_ref, xp_ref, xn_ref, mod_ref, g_ref, win_ref, cw_ref, cb_ref, wout_ref, o_ref, *,
                  seq_len, row_base, tpb):
    i = pl.program_id(0)
    tm, d = x_ref.shape
    r = _mod_row(i, row_base, tpb)
    x = x_ref[...]
    xcat = jnp.concatenate([xp_ref[...], x, xn_ref[...]], axis=0)
    h = _modnorm(xcat, g_ref[...], _mod_chunk(mod_ref, r, SC1, d), _mod_chunk(mod_ref, r, SH1, d)).astype(BF16)
    cx = _dot(h, win_ref[:, d:2 * d]) * _dot(h, win_ref[:, 2 * d:3 * d])
    rows = tm + 2 * HALO
    inner = slice(HALO, HALO + tm)
    starts, ends, (start_live, end_live) = _seq_edges(i, tm, seq_len)
    dn = _zero_rows(pltpu.roll(cx, 1, 0)[inner], starts, start_live)
    up = _zero_rows(pltpu.roll(cx, rows - 1, 0)[inner], ends, end_live)
    cw = cw_ref[...]
    u = dn * cw[0:1, :] + cx[inner] * cw[1:2, :] + up * cw[2:3, :] + cb_ref[...]
    z = (_dot(h[inner], win_ref[:, 0:d]) * u).astype(BF16)
    o_ref[...] = x + _mod_chunk(mod_ref, r, G1, d) * _dot(z, wout_ref[...])


def _sconv(x, mod, g, w_in, conv_w, conv_b, w_out, layer, *, seq_len, row_base, rows_per_batch):
    n, d = x.shape
    assert seq_len & (seq_len - 1) == 0
    tm = min(1024, n if rows_per_batch is None else rows_per_batch)
    tpb = None if rows_per_batch is None else rows_per_batch // tm
    hb = tm // HALO
    last = n // HALO - 1
    return pl.pallas_call(
        functools.partial(_sconv_kernel, seq_len=seq_len, row_base=row_base, tpb=tpb),
        grid=(n // tm,),
        in_specs=[pl.BlockSpec((tm, d), lambda i: (i, 0)),
                  pl.BlockSpec((HALO, d), lambda i: (jnp.maximum(i * hb - 1, 0), 0)),
                  pl.BlockSpec((HALO, d), lambda i: (jnp.minimum((i + 1) * hb, last), 0)),
                  pl.BlockSpec(mod.shape, lambda i: (0, 0)),
                  pl.BlockSpec((1, d), lambda i: (0, 0)),
                  _resident((None, d, 3 * d), lambda i: (layer, 0, 0)),
                  pl.BlockSpec((3, d), lambda i: (0, 0)),
                  pl.BlockSpec((1, d), lambda i: (0, 0)),
                  _resident((None, d, d), lambda i: (layer, 0, 0))],
        out_specs=pl.BlockSpec((tm, d), lambda i: (i, 0)),
        out_shape=jax.ShapeDtypeStruct((n, d), F32),
        compiler_params=_cparams(1),
        name="sconv",
    )(x, x, x, mod, g.reshape(1, d), w_in, conv_w, conv_b.reshape(1, d), w_out)


def _dft_constants(m):
    k = np.arange(m)[:, None]
    s = np.arange(m)[None, :]
    ang = np.pi * ((k * s) % (2 * m)) / m
    rows = 2 * m + NYQ_ROWS
    alt_k = (-1.0) ** np.arange(m)[:, None]
    alt_s = (-1.0) ** np.arange(m)

    def mat(re, im, drop_first):
        out = np.zeros((rows, m))
        out[:m], out[m:2 * m], out[2 * m] = re, im, alt_s
        if drop_first:
            out[:, 0] = 0.0
        return out

    cos, sin = np.cos(ang), np.sin(ang)
    mats = np.stack([mat(cos, -sin, False), mat(alt_k * cos, -alt_k * sin, True),
                     mat(cos, sin, False), mat(alt_k * cos, alt_k * sin, True)])
    ck = np.full((1, m), 2.0)
    ck[0, 0] = 1.0
    minv = np.concatenate([ck * cos.T, -ck * sin.T], axis=1) / (2 * m)
    return jnp.asarray(mats, F32).astype(BF16), jnp.asarray(minv, F32).astype(BF16)


FWD0, FWD1, BWD0, BWD1 = range(4)


def _filter_features(n, n_emb_pad):
    bands = (33 - 1) // 2
    t01 = np.linspace(0.0, 1.0, n)[:, None]
    w = 2.0 * np.pi * np.arange(n)[:, None] / n
    fb = np.linspace(1e-4, bands - 1, bands)[None, :]
    z = np.concatenate([t01, np.cos(fb * w), -np.sin(fb * w)], axis=-1)
    out = np.zeros((n, n_emb_pad))
    out[:, :z.shape[1]] = z
    return jnp.asarray(out, F32)


def _hy_filter_kernel(z_ref, w0_ref, b0_ref, w1_ref, b1_ref, w2_ref, b2_ref, fr_ref, w3b_ref, w3f_ref,
                      dl_ref, mats_ref, g_ref, a3_ref, f_ref, *, n, m):
    c = pl.program_id(0)
    dstep = pl.program_id(1)
    nb = n // m

    @pl.when(jnp.logical_and(c == 0, dstep == 0))
    def _():
        fr = fr_ref[...]
        a = jnp.sin(fr * (_dot(z_ref[...].astype(BF16), w0_ref[...]) + b0_ref[...]))
        a = jnp.sin(fr * (_dot(a.astype(BF16), w1_ref[...]) + b1_ref[...]))
        a = jnp.sin(fr * (_dot(a.astype(BF16), w2_ref[...]) + b2_ref[...]))
        a3_ref[...] = a.astype(BF16)

    @pl.when(dstep == 0)
    def _():
        t = lax.broadcasted_iota(jnp.int32, (n, 1), 0)
        decay = jnp.exp(-(t.astype(F32) / (n - 1.0)) * dl_ref[...])
        a3 = a3_ref[...]
        hb = jnp.where(t == 0, 0.0, _dot(a3, w3b_ref[...]) * decay)
        hf = _dot(a3, w3f_ref[...]) * decay
        norm = (jnp.sum(jnp.abs(hb), axis=0, keepdims=True)
                + jnp.sum(jnp.abs(hf), axis=0, keepdims=True))
        f_ref[0:n, :] = (hb / norm).astype(BF16)
        f_ref[n:2 * n, :] = (hf / norm).astype(BF16)

    d = dstep - (nb - 1)
    ia = jnp.where(d >= 0, FWD0, BWD0)
    ra = jnp.where(d >= 0, n + d * m, -d * m)
    ib = jnp.where(d >= 1, FWD1, jnp.where(d == 0, BWD0, BWD1))
    rb = jnp.where(d >= 1, n + (d - 1) * m, jnp.where(d == 0, 0, (-d - 1) * m))
    ra = pl.multiple_of(ra, m)
    rb = pl.multiple_of(rb, m)
    g = _dot(mats_ref[ia], f_ref[pl.ds(ra, m), :]) + _dot(mats_ref[ib], f_ref[pl.ds(rb, m), :])
    g_ref[0] = g.astype(g_ref.dtype)


def _hy_filter(n, m, w0, b0, w1, b1, w2, b2, w3, freq, mats):
    d = w3.shape[1] // 2
    order = w1.shape[0]
    nb = n // m
    ct = _tile(d, 256)
    emb_pad = 128
    z = _filter_features(n, emb_pad)
    w0p = jnp.zeros((emb_pad, order), F32).at[:w0.shape[0]].set(w0).astype(BF16)
    lo, hi = math.log(HY_TARGET) / HY_SLOW, math.log(HY_TARGET) / HY_FAST
    deltas = jnp.asarray(np.abs(np.linspace(lo, hi, d))[None, :], F32)
    w3b = w3.astype(BF16)
    rows = 2 * m + NYQ_ROWS
    full = lambda shp: pl.BlockSpec(shp, lambda c, s: (0,) * len(shp))
    return pl.pallas_call(
        functools.partial(_hy_filter_kernel, n=n, m=m),
        grid=(d // ct, 2 * nb - 1),
        in_specs=[full((n, emb_pad)), full((emb_pad, order)), full((1, order)),
                  full((order, order)), full((1, order)), full((order, order)), full((1, order)),
                  full((1, order)),
                  pl.BlockSpec((order, ct), lambda c, s: (0, d // ct + c)),
                  pl.BlockSpec((order, ct), lambda c, s: (0, c)),
                  pl.BlockSpec((1, ct), lambda c, s: (0, c)),
                  _resident((4, rows, m), lambda c, s: (0, 0, 0))],
        out_specs=pl.BlockSpec((1, rows, ct), lambda c, s: (s, 0, c)),
        out_shape=jax.ShapeDtypeStruct((2 * nb - 1, rows, d), BF16),
        scratch_shapes=[pltpu.VMEM((n, order), BF16), pltpu.VMEM((2 * n, ct), BF16)],
        compiler_params=_cparams(2),
        name="hyena_filter",
    )(z, w0p, b0.reshape(1, order), w1.astype(BF16), b1.reshape(1, order), w2.astype(BF16),
      b2.reshape(1, order), freq.reshape(1, order), w3b, w3b, deltas, mats)


def _hyena_in_kernel(x_ref, xp_ref, xn_ref, mod_ref, g_ref, win_ref, cw_ref, cb_ref, x0_ref, v_ref, *,
                     seq_len, row_base, tpb, ct):
    i = pl.program_id(0)
    tm, d = x_ref.shape
    r = _mod_row(i, row_base, tpb)
    xcat = jnp.concatenate([xp_ref[...], x_ref[...], xn_ref[...]], axis=0)
    h = _modnorm(xcat, g_ref[...], _mod_chunk(mod_ref, r, SC1, d), _mod_chunk(mod_ref, r, SH1, d)).astype(BF16)
    rows = tm + 2 * HALO
    inner = slice(HALO, HALO + tm)
    starts, ends, (start_live, end_live) = _seq_edges(i, tm, seq_len)

    def conv(col):
        p = _dot(h, win_ref[:, col:col + ct])
        w = cw_ref[:, col:col + ct]
        dn = _zero_rows(pltpu.roll(p, 1, 0)[inner], starts, start_live)
        up = _zero_rows(pltpu.roll(p, rows - 1, 0)[inner], ends, end_live)
        return dn * w[0:1, :] + p[inner] * w[1:2, :] + up * w[2:3, :] + cb_ref[:, col:col + ct]

    for c in range(0, d, ct):
        x0_ref[:, c:c + ct] = conv(c).astype(BF16)
        v_ref[:, c:c + ct] = (conv(2 * d + c) * conv(d + c)).astype(BF16)


def _hyena_in(x, mod, g, w_in, layer, conv_w, conv_b, *, seq_len, row_base, rows_per_batch):
    n, d = x.shape
    assert seq_len & (seq_len - 1) == 0
    tm = min(1024, n if rows_per_batch is None else rows_per_batch)
    tpb = None if rows_per_batch is None else rows_per_batch // tm
    hb = tm // HALO
    last = n // HALO - 1
    out = pl.BlockSpec((tm, d), lambda i: (i, 0))
    return pl.pallas_call(
        functools.partial(_hyena_in_kernel, seq_len=seq_len, row_base=row_base, tpb=tpb, ct=_tile(d, 256)),
        grid=(n // tm,),
        in_specs=[pl.BlockSpec((tm, d), lambda i: (i, 0)),
                  pl.BlockSpec((HALO, d), lambda i: (jnp.maximum(i * hb - 1, 0), 0)),
                  pl.BlockSpec((HALO, d), lambda i: (jnp.minimum((i + 1) * hb, last), 0)),
                  pl.BlockSpec(mod.shape, lambda i: (0, 0)),
                  pl.BlockSpec((1, d), lambda i: (0, 0)),
                  _resident((None, d, 3 * d), lambda i: (layer, 0, 0)),
                  pl.BlockSpec((3, 3 * d), lambda i: (0, 0)),
                  pl.BlockSpec((1, 3 * d), lambda i: (0, 0))],
        out_specs=[out, out],
        out_shape=[jax.ShapeDtypeStruct((n, d), BF16), jax.ShapeDtypeStruct((n, d), BF16)],
        compiler_params=_cparams(1),
        name="hyena_in",
    )(x, x, x, mod, g.reshape(1, d), w_in, conv_w, conv_b.reshape(1, 3 * d))


def _hy_conv_kernel(x0_ref, v_ref, hb_ref, g_ref, ma_ref, minv_ref, o_ref, vf_ref, *, m, nb):
    i = pl.program_id(2)

    @pl.when(i == 0)
    def _():
        for j in range(nb):
            vf_ref[j] = _dot(ma_ref[...], v_ref[j * m:(j + 1) * m, :])

    ct = o_ref.shape[1]
    acc_r = jnp.zeros((m, ct), F32)
    acc_i = jnp.zeros((m, ct), F32)
    acc_n = jnp.zeros((NYQ_ROWS, ct), F32)
    for j in range(nb):
        g = g_ref[i - j + nb - 1].astype(F32)
        vr = vf_ref[j, 0:m, :]
        vi = vf_ref[j, m:2 * m, :]
        gr = g[0:m, :]
        gi = g[m:2 * m, :]
        acc_r += gr * vr - gi * vi
        acc_i += gr * vi + gi * vr
        acc_n += g[2 * m:, :] * vf_ref[j, 2 * m:, :]
    spec = jnp.concatenate([acc_r, acc_i], axis=0).astype(BF16)
    t = lax.broadcasted_iota(jnp.int32, (m, 1), 0)
    sgn = (1 - 2 * jnp.bitwise_and(t, 1)).astype(F32) * (0.5 / m)
    y = _dot(minv_ref[...], spec) + sgn * acc_n[0:1, :]
    start = pl.multiple_of(i * m, m)
    v1 = v_ref[pl.ds(start, m), :].astype(F32)
    x0 = x0_ref[pl.ds(start, m), :].astype(F32)
    o_ref[...] = ((y + v1 * hb_ref[...]) * x0).astype(o_ref.dtype)


def _hy_conv_short_kernel(x0_ref, v_ref, hb_ref, g_ref, ma_ref, minv_ref, o_ref, *, m, spb):
    g = g_ref[0].astype(F32)
    gr, gi, gn = g[0:m, :], g[m:2 * m, :], g[2 * m:, :]
    t = lax.broadcasted_iota(jnp.int32, (m, 1), 0)
    sgn = (1 - 2 * jnp.bitwise_and(t, 1)).astype(F32) * (0.5 / m)
    for s in range(spb):
        rows = slice(s * m, (s + 1) * m)
        vb = v_ref[rows, :]
        vf = _dot(ma_ref[...], vb)
        vr, vi = vf[0:m, :], vf[m:2 * m, :]
        spec = jnp.concatenate([gr * vr - gi * vi, gr * vi + gi * vr], axis=0).astype(BF16)
        y = _dot(minv_ref[...], spec) + sgn * (gn * vf[2 * m:, :])[0:1, :]
        o_ref[rows, :] = ((y + vb.astype(F32) * hb_ref[...]) * x0_ref[rows, :].astype(F32)).astype(o_ref.dtype)


def _hy_conv(x0, v, hy_bias, g, ma, minv, *, n, m, batch):
    d = hy_bias.shape[0]
    nb = n // m
    ct = _tile(d, 256)
    rows = 2 * m + NYQ_ROWS
    if nb == 1:
        spb = max(s for s in range(1, 9) if batch % s == 0)
        blk = pl.BlockSpec((spb * n, ct), lambda c, b: (b, c))
        return pl.pallas_call(
            functools.partial(_hy_conv_short_kernel, m=m, spb=spb),
            grid=(d // ct, batch // spb),
            in_specs=[blk, blk,
                      pl.BlockSpec((1, ct), lambda c, b: (0, c)),
                      pl.BlockSpec((1, rows, ct), lambda c, b: (0, 0, c)),
                      _resident((rows, m), lambda c, b: (0, 0)),
                      _resident((m, 2 * m), lambda c, b: (0, 0))],
            out_specs=blk,
            out_shape=jax.ShapeDtypeStruct((batch * n, d), BF16),
            compiler_params=_cparams(2),
            name="hyena_conv_short",
        )(x0, v, hy_bias.reshape(1, d), g, ma, minv)
    seq = pl.BlockSpec((n, ct), lambda c, b, i: (b, c))
    return pl.pallas_call(
        functools.partial(_hy_conv_kernel, m=m, nb=nb),
        grid=(d // ct, batch, nb),
        in_specs=[seq, seq,
                  pl.BlockSpec((1, ct), lambda c, b, i: (0, c)),
                  _resident((2 * nb - 1, rows, ct), lambda c, b, i: (0, 0, c)),
                  _resident((rows, m), lambda c, b, i: (0, 0)),
                  _resident((m, 2 * m), lambda c, b, i: (0, 0))],
        out_specs=pl.BlockSpec((m, ct), lambda c, b, i: (b * nb + i, c)),
        out_shape=jax.ShapeDtypeStruct((batch * n, d), BF16),
        scratch_shapes=[pltpu.VMEM((nb, rows, ct), F32)],
        compiler_params=_cparams(3),
        name="hyena_conv",
    )(x0, v, hy_bias.reshape(1, d), g, ma, minv)


def _rope_tables(n):
    axis = ROPE_DIM // 2
    half = axis // 2
    inv = ROPE_THETA ** (-np.arange(0, axis, 2) / axis)
    t = np.arange(n)
    row, col = t // GRID_W, t % GRID_W
    cos = np.ones((n, HEAD_SLOT))
    sin_a = np.zeros((n, HEAD_SLOT))
    sin_b = np.zeros((n, HEAD_SLOT))
    for k, pos in enumerate((row, col)):
        ang = pos[:, None] * inv[None, :]
        base = QK_NOPE + k * axis
        cos[:, base:base + half] = np.cos(ang)
        cos[:, base + half:base + axis] = np.cos(ang)
        sin_a[:, base:base + half] = -np.sin(ang)
        sin_b[:, base + half:base + axis] = np.sin(ang)
    return jnp.asarray(cos.T, F32), jnp.asarray(sin_a.T, F32), jnp.asarray(sin_b.T, F32)


def _expand_keys(ckv_b, kpe_slot, wukv_ref, kng, k_ref, v_ref):
    kw = N_HEADS * HEAD_SLOT
    kv = _dot(ckv_b, wukv_ref[...])
    for h in range(N_HEADS):
        sl = slice(h * HEAD_SLOT, (h + 1) * HEAD_SLOT)
        x = kv[:, sl] + kpe_slot
        scale = lax.rsqrt(jnp.sum(x * x, axis=-1, keepdims=True) * (1.0 / QK_DIM) + EPS)
        k_ref[:, sl] = (x * scale * kng).astype(BF16)
    v_ref[...] = kv[:, kw:].T.astype(BF16)


def _rmsnorm_rows(x, g_col):
    ms = jnp.mean(x * x, axis=0, keepdims=True)
    return x * lax.rsqrt(ms + EPS) * g_col


def _rope_gains(g3, rope):
    if rope is None:
        return g3[:, 0:1], None, None
    cos, sin_a, sin_b = rope
    return g3[:, 0:1] * cos, g3[:, 1:2] * sin_a, g3[:, 2:3] * sin_b


def _partner_term(x, ga, gb):
    half = ROPE_DIM // 4
    return pltpu.roll(x, HEAD_SLOT - half, 0) * ga + pltpu.roll(x, half, 0) * gb


def _slab_norm_rope(x, gc, partner):
    scale = lax.rsqrt(jnp.sum(x * x, axis=0, keepdims=True) * (1.0 / QK_DIM) + EPS)
    y = x * gc
    if partner is not None:
        y = y + partner
    return y * scale


def _mla_prep_kernel(*refs, row_base, tpb, use_rope, emit_latent):
    (x_ref, mod_ref, g_ref, wcat_ref, qg_ref, kvg_ref, wuq_ref, wukv_ref, qng_ref, kng_ref), refs = refs[:10], refs[10:]
    if use_rope:
        (cos_ref, sa_ref, sb_ref), refs = refs[:3], refs[3:]
        rope = (cos_ref[...], sa_ref[...], sb_ref[...])
    else:
        rope = None
    q_ref, k_ref, v_ref = refs[:3]
    i = pl.program_id(0)
    d = x_ref.shape[1]
    r = _mod_row(i, row_base, tpb)
    h = _modnorm(x_ref[...], g_ref[...], _mod_chunk(mod_ref, r, SC1, d), _mod_chunk(mod_ref, r, SH1, d))
    at = _dot(wcat_ref[...], h.T.astype(BF16))
    q_rank = qg_ref.shape[0]
    kv_rank = kvg_ref.shape[0]
    qct = _rmsnorm_rows(at[:q_rank], qg_ref[...]).astype(BF16)
    ckvt = _rmsnorm_rows(at[q_rank:q_rank + kv_rank], kvg_ref[...])
    kpet = at[q_rank + kv_rank:]
    if emit_latent:
        refs[3][...] = ckvt.T
        refs[4][...] = kpet.T
    qt = _dot(wuq_ref[...], qct)
    qc, qa, qb = _rope_gains(qng_ref[...] * (QK_DIM ** -0.5 * math.log2(math.e)), rope)
    for hd in range(N_HEADS):
        sl = slice(hd * HEAD_SLOT, (hd + 1) * HEAD_SLOT)
        x = qt[sl]
        q_ref[sl, :] = _slab_norm_rope(x, qc, _partner_term(x, qa, qb) if use_rope else None).astype(BF16)
    kw = N_HEADS * HEAD_SLOT
    kvt = _dot(wukv_ref[...], ckvt.astype(BF16))
    v_ref[...] = kvt[kw:].astype(BF16)
    kc, ka, kb = _rope_gains(kng_ref[...], rope)
    partner = _partner_term(kpet, ka, kb) if use_rope else None
    for hd in range(N_HEADS):
        sl = slice(hd * HEAD_SLOT, (hd + 1) * HEAD_SLOT)
        k_ref[:, sl] = _slab_norm_rope(kvt[sl] + kpet, kc, partner).T.astype(BF16)


def _mla_prep(x, mod, g, wcat_t, qg, kvg, wuq_t, wukv_t, qng3, kng3, rope_t, *, row_base, rows_per_batch,
              emit_latent):
    n, d = x.shape
    seq = n if rows_per_batch is None else rows_per_batch
    tm = min(512, seq)
    tpb = None if rows_per_batch is None else rows_per_batch // tm
    kw = N_HEADS * HEAD_SLOT
    vw = N_HEADS * V_DIM
    q_rank, kv_rank = qg.shape[0], kvg.shape[0]
    full = lambda a: pl.BlockSpec(a.shape, lambda i: (0,) * a.ndim)
    args = [x, mod, g.reshape(1, d), wcat_t, qg.reshape(-1, 1), kvg.reshape(-1, 1), wuq_t, wukv_t, qng3, kng3]
    in_specs = [pl.BlockSpec((tm, d), lambda i: (i, 0))] + [full(a) for a in args[1:]]
    if rope_t is not None:
        pt = rows_per_batch // tm
        args += list(rope_t)
        in_specs += [pl.BlockSpec((HEAD_SLOT, tm), lambda i: (0, i % pt))] * 3
    row = lambda w: pl.BlockSpec((tm, w), lambda i: (i, 0))
    col = lambda w: pl.BlockSpec((w, tm), lambda i: (0, i))
    out_specs = [col(kw), row(kw), col(vw)]
    out_shape = [jax.ShapeDtypeStruct((kw, n), BF16), jax.ShapeDtypeStruct((n, kw), BF16),
                 jax.ShapeDtypeStruct((vw, n), BF16)]
    if emit_latent:
        out_specs += [row(kv_rank), row(HEAD_SLOT)]
        out_shape += [jax.ShapeDtypeStruct((n, kv_rank), F32), jax.ShapeDtypeStruct((n, HEAD_SLOT), F32)]
    return pl.pallas_call(
        functools.partial(_mla_prep_kernel, row_base=row_base, tpb=tpb, use_rope=rope_t is not None,
                          emit_latent=emit_latent),
        grid=(n // tm,),
        in_specs=in_specs, out_specs=out_specs, out_shape=out_shape,
        compiler_params=_cparams(1),
        name="mla_prep",
    )(*args)


def _mla_cache_kernel(ckv_ref, kpe_ref, wukv_ref, kng_ref, k_ref, v_ref):
    _expand_keys(ckv_ref[...].astype(BF16), kpe_ref[...], wukv_ref, kng_ref[...], k_ref, v_ref)


def _mla_cache(ckv, kpe_slot, wukv, kng):
    n = ckv.shape[0]
    tm = min(512, n)
    kw = N_HEADS * HEAD_SLOT
    vw = N_HEADS * V_DIM
    row = lambda w: pl.BlockSpec((tm, w), lambda i: (i, 0))
    return pl.pallas_call(
        _mla_cache_kernel,
        grid=(n // tm,),
        in_specs=[row(ckv.shape[1]), row(HEAD_SLOT), pl.BlockSpec(wukv.shape, lambda i: (0, 0)),
                  pl.BlockSpec(kng.shape, lambda i: (0, 0))],
        out_specs=[row(kw), pl.BlockSpec((vw, tm), lambda i: (0, i))],
        out_shape=[jax.ShapeDtypeStruct((n, kw), BF16), jax.ShapeDtypeStruct((vw, n), BF16)],
        compiler_params=_cparams(1),
        name="mla_cache_keys",
    )(ckv, kpe_slot, wukv, kng)


KEY_CHUNK = 256


def _attn_kernel(*refs, has_cache, heads):
    if has_cache:
        q_ref, k_ref, v_ref, kc_ref, vc_ref, o_ref = refs
    else:
        q_ref, k_ref, v_ref, o_ref = refs
    n = k_ref.shape[0]
    outs = []
    for hh in range(heads):
        sl = slice(hh * HEAD_SLOT, (hh + 1) * HEAD_SLOT)
        vs = slice(hh * V_DIM, (hh + 1) * V_DIM)
        q = q_ref[sl, :]
        pieces = [(k_ref, v_ref, c, min(KEY_CHUNK, n - c)) for c in range(0, n, KEY_CHUNK)]
        if has_cache:
            pieces.append((kc_ref, vc_ref, 0, kc_ref.shape[0]))
        sts = [_dot(kr[c:c + w, sl], q) for kr, _, c, w in pieces]
        mx = functools.reduce(jnp.maximum, [jnp.max(st, axis=0, keepdims=True) for st in sts])
        pts = [jnp.exp2(st - mx) for st in sts]
        l = functools.reduce(jnp.add, [jnp.sum(pt, axis=0, keepdims=True) for pt in pts])
        ot = functools.reduce(jnp.add, [_dot(vr[vs, c:c + w], pt.astype(BF16))
                                        for (_, vr, c, w), pt in zip(pieces, pts)])
        outs.append(ot / l)
    o_ref[...] = jnp.concatenate(outs, axis=0).T.astype(o_ref.dtype)


def _attention(q, k, v, kc, vc, *, batch, n, past):
    has_cache = kc is not None
    tq = min(512, n)
    heads = 2 if n > tq else N_HEADS
    nq = n // tq
    groups = N_HEADS // heads
    in_specs = [pl.BlockSpec((heads * HEAD_SLOT, tq), lambda b, j, i: (j, b * nq + i)),
                pl.BlockSpec((n, heads * HEAD_SLOT), lambda b, j, i: (b, j)),
                pl.BlockSpec((heads * V_DIM, n), lambda b, j, i: (j, b))]
    args = [q, k, v]
    if has_cache:
        in_specs += [pl.BlockSpec((past, heads * HEAD_SLOT), lambda b, j, i: (b, j)),
                     pl.BlockSpec((heads * V_DIM, past), lambda b, j, i: (j, b))]
        args += [kc, vc]
    return pl.pallas_call(
        functools.partial(_attn_kernel, has_cache=has_cache, heads=heads),
        grid=(batch, groups, nq),
        in_specs=in_specs,
        out_specs=pl.BlockSpec((tq, heads * V_DIM), lambda b, j, i: (b * nq + i, j)),
        out_shape=jax.ShapeDtypeStruct((batch * n, N_HEADS * V_DIM), BF16),
        compiler_params=_cparams(3),
        name="attention",
    )(*args)


def _pad_heads(w, width):
    k = w.shape[0]
    w = w.reshape(k, N_HEADS, width)
    return jnp.pad(w, ((0, 0), (0, 0), (0, HEAD_SLOT - width))).reshape(k, N_HEADS * HEAD_SLOT)


def _slot_gain(g):
    return jnp.pad(g, (0, HEAD_SLOT - QK_DIM)).reshape(1, HEAD_SLOT)


def _gain_columns(g_row):
    g = g_row.reshape(HEAD_SLOT)
    half = ROPE_DIM // 4
    return jnp.stack([g, jnp.roll(g, -half), jnp.roll(g, half)], axis=1)


def kernel(x_prompt, x_sample, cache_ckv, cache_kpe, c, c_ctx, norm1_g, norm2_g, mod_w, mod_b, mlp_w1, mlp_w2, sc_w_in, sc_conv_w, sc_conv_b, sc_w_out, mla_w_dq, mla_q_norm_g, mla_w_uq, mla_w_dkv, mla_kv_norm_g, mla_w_ukv, mla_qn_g, mla_kn_g, mla_w_o, hy_w_in, hy_conv_w, hy_conv_b, hy_f_w0, hy_f_b0, hy_f_w1, hy_f_b1, hy_f_w2, hy_f_b2, hy_f_w3, hy_sin_freq, hy_bias, hy_w_out):
    batch, seq, d = x_prompt.shape
    dec_batch, dec_seq, _ = x_sample.shape
    depth = mod_w.shape[0]
    n_mla = mla_w_dq.shape[0]
    past = cache_ckv.shape[2]
    kv_rank = mla_kv_norm_g.shape[1]
    assert 1 + dec_batch <= MOD_ROWS

    cond = jnp.zeros((MOD_ROWS, d), F32).at[0].set(c_ctx).at[1:1 + dec_batch].set(c)
    mods = _modulation(cond, mod_w, mod_b)

    w1 = mlp_w1.astype(BF16)
    w2 = mlp_w2.astype(BF16)
    sc_in = sc_w_in.astype(BF16)
    sc_out = sc_w_out.astype(BF16)
    hy_in = hy_w_in.astype(BF16)
    hy_out = hy_w_out.astype(BF16)
    mla_o = mla_w_o.astype(BF16)

    streams = [
        dict(x=x_prompt.reshape(batch * seq, d), row_base=0, rpb=None, seq=seq, nseq=batch, ctx=True),
        dict(x=x_sample.reshape(dec_batch * dec_seq, d), row_base=1, rpb=dec_seq, seq=dec_seq,
             nseq=dec_batch, ctx=False),
    ]
    hy_consts = {}
    new_ckv, new_kpe = [], []

    for i in range(depth):
        kind, j = i % 3, i // 3
        mod = mods[i]
        for st in streams:
            x = st["x"]
            rb, rpb, n = st["row_base"], st["rpb"], st["seq"]
            tail = None
            if kind == 0:
                x = _sconv(x, mod, norm1_g[i], sc_in, sc_conv_w[j], sc_conv_b[j], sc_out, j, seq_len=n,
                           row_base=rb, rows_per_batch=rpb)
            elif kind == 1:
                pe = mla_w_dkv[j][:, kv_rank:]
                pe_slot = jnp.pad(pe, ((0, 0), (QK_NOPE, HEAD_SLOT - QK_DIM)))
                wcat = jnp.concatenate([mla_w_dq[j], mla_w_dkv[j][:, :kv_rank], pe_slot], axis=1).astype(BF16)
                wuq = _pad_heads(mla_w_uq[j], QK_DIM).astype(BF16)
                ukv = mla_w_ukv[j].reshape(kv_rank, N_HEADS, QK_NOPE + V_DIM)
                wukv = jnp.concatenate(
                    [_pad_heads(ukv[:, :, :QK_NOPE].reshape(kv_rank, -1), QK_NOPE),
                     ukv[:, :, QK_NOPE:].reshape(kv_rank, -1)], axis=1).astype(BF16)
                qng, kng = _slot_gain(mla_qn_g[j]), _slot_gain(mla_kn_g[j])
                rope = None if st["ctx"] else _rope_tables(n)
                outs = _mla_prep(x, mod, norm1_g[i], wcat.T, mla_q_norm_g[j], mla_kv_norm_g[j], wuq.T, wukv.T,
                                 _gain_columns(qng), _gain_columns(kng), rope, row_base=rb,
                                 rows_per_batch=rpb, emit_latent=st["ctx"])
                q, k, v = outs[:3]
                if st["ctx"]:
                    new_ckv.append(outs[3].reshape(batch, seq, kv_rank))
                    new_kpe.append(outs[4][:, QK_NOPE:QK_DIM].reshape(batch, seq, ROPE_DIM))
                    kc = vc = None
                else:
                    ck = cache_ckv[:, j].reshape(dec_batch * past, kv_rank)
                    cp = jnp.pad(cache_kpe[:, j].reshape(dec_batch * past, ROPE_DIM),
                                 ((0, 0), (QK_NOPE, HEAD_SLOT - QK_DIM)))
                    kc, vc = _mla_cache(ck, cp, wukv, kng)
                o = _attention(q, k, v, kc, vc, batch=st["nseq"], n=n, past=past)
                tail = (o, mla_o, j)
            else:
                m = min(HY_BLOCK, n)
                if m not in hy_consts:
                    hy_consts[m] = _dft_constants(m)
                mats, minv = hy_consts[m]
                g = _hy_filter(n, m, hy_f_w0[j], hy_f_b0[j], hy_f_w1[j], hy_f_b1[j], hy_f_w2[j],
                               hy_f_b2[j], hy_f_w3[j], hy_sin_freq[j], mats)
                x0, v1 = _hyena_in(x, mod, norm1_g[i], hy_in, j, hy_conv_w[j], hy_conv_b[j], seq_len=n,
                                   row_base=rb, rows_per_batch=rpb)
                z = _hy_conv(x0, v1, hy_bias[j], g, mats[FWD0], minv, n=n, m=m, batch=st["nseq"])
                tail = (z, hy_out, j)
            x = _mlp(x, mod, norm2_g[i], w1, w2, i, row_base=rb, rows_per_batch=rpb, tail=tail)
            st["x"] = x

    y_prompt = streams[0]["x"].reshape(batch, seq, d)
    y_sample = streams[1]["x"].reshape(dec_batch, dec_seq, d)
    return (y_prompt, y_sample, jnp.stack(new_ckv, axis=1), jnp.stack(new_kpe, axis=1))
```

```python
import functools
import math

import numpy as np
import jax
import jax.numpy as jnp
from jax import lax
from jax.experimental import pallas as pl
from jax.experimental.pallas import tpu as pltpu

F32 = jnp.float32
BF16 = jnp.bfloat16

EPS = 1e-6
MOD_CHUNKS = 6
N_HEADS = 16
QK_NOPE = 64
ROPE_DIM = 32
QK_DIM = QK_NOPE + ROPE_DIM
V_DIM = 64
HEAD_SLOT = 128
GRID_W = 64
ROPE_THETA = 10000.0
HY_TARGET = 1e-2
HY_FAST = 0.3
HY_SLOW = 1.5
HY_BLOCK = 1024
NYQ_ROWS = 16
MOD_ROWS = 8

V7X_VMEM_BYTES = 64 * 1024 * 1024
VMEM_LIMIT = V7X_VMEM_BYTES - 8 * 1024 * 1024

SH1, SC1, G1, SH2, SC2, G2 = range(6)


def _tile(n, pref, align=128):
    if n <= pref:
        return n
    t = pref - pref % align
    while n % t:
        t -= align
    return t


def _cparams(n_axes):
    return pltpu.CompilerParams(dimension_semantics=("arbitrary",) * n_axes,
                                vmem_limit_bytes=VMEM_LIMIT)


def _mod_chunk(mod_ref, r, k, d):
    return mod_ref[pl.ds(r, 1), k * d:(k + 1) * d]


def _mod_row(i, row_base, tiles_per_batch):
    if tiles_per_batch is None:
        return row_base
    return row_base + i // tiles_per_batch


def _modnorm(x, g, sc, sh):
    ms = jnp.mean(x * x, axis=-1, keepdims=True)
    return x * lax.rsqrt(ms + EPS) * (g * (1.0 + sc)) + sh


def _dot(a, b):
    return jnp.dot(a, b, preferred_element_type=F32)


def _mod_kernel(c_ref, w_ref, b_ref, o_ref):
    c = c_ref[...]
    s = (c * jax.nn.sigmoid(c)).astype(BF16)
    o_ref[0] = _dot(s, w_ref[0].astype(BF16)) + b_ref[0]


def _modulation(cond, mod_w, mod_b):
    depth, d, n6 = mod_w.shape
    tn = _tile(n6, 1536)
    return pl.pallas_call(
        _mod_kernel,
        grid=(depth, n6 // tn),
        in_specs=[pl.BlockSpec((MOD_ROWS, d), lambda l, j: (0, 0)),
                  pl.BlockSpec((1, d, tn), lambda l, j: (l, 0, j)),
                  pl.BlockSpec((1, 1, tn), lambda l, j: (l, 0, j))],
        out_specs=pl.BlockSpec((1, MOD_ROWS, tn), lambda l, j: (l, 0, j)),
        out_shape=jax.ShapeDtypeStruct((depth, MOD_ROWS, n6), F32),
        compiler_params=_cparams(2),
        name="modulation",
    )(cond, mod_w, mod_b.reshape(depth, 1, n6))


def _resident(shape, index_map):
    return pl.BlockSpec(shape, index_map, pipeline_mode=pl.Buffered(1))


def _mlp_kernel(*refs, row_base, tpb, tf, mixer_tail):
    if mixer_tail:
        z_ref, wo_ref, refs = refs[0], refs[1], refs[2:]
    x_ref, mod_ref, g_ref, w1_ref, w2_ref, o_ref = refs
    d = x_ref.shape[1]
    r = _mod_row(pl.program_id(0), row_base, tpb)
    x = x_ref[...]
    if mixer_tail:
        x = x + _mod_chunk(mod_ref, r, G1, d) * _dot(z_ref[...], wo_ref[...])
    h = _modnorm(x, g_ref[...], _mod_chunk(mod_ref, r, SC2, d), _mod_chunk(mod_ref, r, SH2, d)).astype(BF16)
    acc = None
    for c in range(0, w1_ref.shape[1], tf):
        a = jnp.maximum(_dot(h, w1_ref[:, c:c + tf]), 0.0)
        y = _dot((a * a).astype(BF16), w2_ref[c:c + tf, :])
        acc = y if acc is None else acc + y
    o_ref[...] = x + _mod_chunk(mod_ref, r, G2, d) * acc


def _mlp(x, mod, g, w1, w2, layer, *, row_base, rows_per_batch, tail=None):
    n, d = x.shape
    dff = w1.shape[2]
    tm = min(1024, n if rows_per_batch is None else rows_per_batch)
    tpb = None if rows_per_batch is None else rows_per_batch // tm
    args = [x, mod, g.reshape(1, d), w1, w2]
    in_specs = [pl.BlockSpec((tm, d), lambda i: (i, 0)),
                pl.BlockSpec(mod.shape, lambda i: (0, 0)),
                pl.BlockSpec((1, d), lambda i: (0, 0)),
                _resident((None, d, dff), lambda i: (layer, 0, 0)),
                _resident((None, dff, d), lambda i: (layer, 0, 0))]
    if tail is not None:
        z, w_out, tail_layer = tail
        k = z.shape[1]
        args = [z, w_out] + args
        in_specs = [pl.BlockSpec((tm, k), lambda i: (i, 0)),
                    _resident((None, k, d), lambda i: (tail_layer, 0, 0))] + in_specs
    return pl.pallas_call(
        functools.partial(_mlp_kernel, row_base=row_base, tpb=tpb, tf=_tile(dff, 1024),
                          mixer_tail=tail is not None),
        grid=(n // tm,),
        in_specs=in_specs,
        out_specs=pl.BlockSpec((tm, d), lambda i: (i, 0)),
        out_shape=jax.ShapeDtypeStruct((n, d), F32),
        compiler_params=_cparams(1),
        name="mlp",
    )(*args)


HALO = 16
SUBLANES = 8


def _seq_edges(i, tm, seq_len):
    if seq_len <= tm:
        return list(range(0, tm, seq_len)), list(range(seq_len - 1, tm, seq_len)), (None, None)
    mask = seq_len - 1
    return [0], [tm - 1], (jnp.bitwise_and(i * tm, mask) == 0, jnp.bitwise_and((i + 1) * tm, mask) == 0)


def _zero_rows(x, rows, live):
    sub = lax.broadcasted_iota(jnp.int32, (SUBLANES, 1), 0)
    pieces, done = [], 0
    for grp in sorted({r // SUBLANES for r in rows}):
        lo = grp * SUBLANES
        if lo > done:
            pieces.append(x[done:lo])
        hit = None
        for r in rows:
            if r // SUBLANES == grp:
                target = r % SUBLANES if live is None else jnp.where(live, r % SUBLANES, -1)
                hit = (sub == target) if hit is None else jnp.logical_or(hit, sub == target)
        pieces.append(jnp.where(hit, 0.0, x[lo:lo + SUBLANES]))
        done = lo + SUBLANES
    if done < x.shape[0]:
        pieces.append(x[done:])
    return jnp.concatenate(pieces, axis=0)


def _sconv_kernel(x_ref, xp_ref, xn_ref, mod_ref, g_ref, win_ref, cw_ref, cb_ref, wout_ref, o_ref, *,
                  seq_len, row_base, tpb):
    i = pl.program_id(0)
    tm, d = x_ref.shape
    r = _mod_row(i, row_base, tpb)
    x = x_ref[...]
    xcat = jnp.concatenate([xp_ref[...], x, xn_ref[...]], axis=0)
    h = _modnorm(xcat, g_ref[...], _mod_chunk(mod_ref, r, SC1, d), _mod_chunk(mod_ref, r, SH1, d)).astype(BF16)
    cx = _dot(h, win_ref[:, d:2 * d]) * _dot(h, win_ref[:, 2 * d:3 * d])
    rows = tm + 2 * HALO
    inner = slice(HALO, HALO + tm)
    starts, ends, (start_live, end_live) = _seq_edges(i, tm, seq_len)
    dn = _zero_rows(pltpu.roll(cx, 1, 0)[inner], starts, start_live)
    up = _zero_rows(pltpu.roll(cx, rows - 1, 0)[inner], ends, end_live)
    cw = cw_ref[...]
    u = dn * cw[0:1, :] + cx[inner] * cw[1:2, :] + up * cw[2:3, :] + cb_ref[...]
    z = (_dot(h[inner], win_ref[:, 0:d]) * u).astype(BF16)
    o_ref[...] = x + _mod_chunk(mod_ref, r, G1, d) * _dot(z, wout_ref[...])


def _sconv(x, mod, g, w_in, conv_w, conv_b, w_out, layer, *, seq_len, row_base, rows_per_batch):
    n, d = x.shape
    assert seq_len & (seq_len - 1) == 0
    tm = min(1024, n if rows_per_batch is None else rows_per_batch)
    tpb = None if rows_per_batch is None else rows_per_batch // tm
    hb = tm // HALO
    last = n // HALO - 1
    return pl.pallas_call(
        functools.partial(_sconv_kernel, seq_len=seq_len, row_base=row_base, tpb=tpb),
        grid=(n // tm,),
        in_specs=[pl.BlockSpec((tm, d), lambda i: (i, 0)),
                  pl.BlockSpec((HALO, d), lambda i: (jnp.maximum(i * hb - 1, 0), 0)),
                  pl.BlockSpec((HALO, d), lambda i: (jnp.minimum((i + 1) * hb, last), 0)),
                  pl.BlockSpec(mod.shape, lambda i: (0, 0)),
                  pl.BlockSpec((1, d), lambda i: (0, 0)),
                  _resident((None, d, 3 * d), lambda i: (layer, 0, 0)),
                  pl.BlockSpec((3, d), lambda i: (0, 0)),
                  pl.BlockSpec((1, d), lambda i: (0, 0)),
                  _resident((None, d, d), lambda i: (layer, 0, 0))],
        out_specs=pl.BlockSpec((tm, d), lambda i: (i, 0)),
        out_shape=jax.ShapeDtypeStruct((n, d), F32),
        compiler_params=_cparams(1),
        name="sconv",
    )(x, x, x, mod, g.reshape(1, d), w_in, conv_w, conv_b.reshape(1, d), w_out)


def _dft_constants(m):
    k = np.arange(m)[:, None]
    s = np.arange(m)[None, :]
    ang = np.pi * ((k * s) % (2 * m)) / m
    rows = 2 * m + NYQ_ROWS
    cos, sin = np.cos(ang), np.sin(ang)

    def mat(re, im):
        out = np.zeros((rows, m))
        out[:m], out[m:2 * m], out[2 * m] = re, im, (-1.0) ** np.arange(m)
        return out

    mats = np.stack([mat(cos, -sin), mat(cos, sin)])
    ck = np.full((1, m), 2.0)
    ck[0, 0] = 1.0
    minv = np.concatenate([ck * cos.T, -ck * sin.T], axis=1) / (2 * m)
    return jnp.asarray(mats, F32).astype(BF16), jnp.asarray(minv, F32).astype(BF16)


FWD0, BWD0 = range(2)


def _filter_features(n, n_emb_pad):
    bands = (33 - 1) // 2
    t01 = np.linspace(0.0, 1.0, n)[:, None]
    w = 2.0 * np.pi * np.arange(n)[:, None] / n
    fb = np.linspace(1e-4, bands - 1, bands)[None, :]
    z = np.concatenate([t01, np.cos(fb * w), -np.sin(fb * w)], axis=-1)
    out = np.zeros((n, n_emb_pad))
    out[:, :z.shape[1]] = z
    return jnp.asarray(out, F32)


def _hy_filter_kernel(z_ref, w0_ref, b0_ref, w1_ref, b1_ref, w2_ref, b2_ref, fr_ref, w3b_ref, w3f_ref,
                      dl_ref, mats_ref, g_ref, a3_ref, f_ref, t_ref, *, n, m):
    c = pl.program_id(0)
    dstep = pl.program_id(1)
    nb = n // m

    @pl.when(jnp.logical_and(c == 0, dstep == 0))
    def _():
        fr = fr_ref[...]
        a = jnp.sin(fr * (_dot(z_ref[...].astype(BF16), w0_ref[...]) + b0_ref[...]))
        a = jnp.sin(fr * (_dot(a.astype(BF16), w1_ref[...]) + b1_ref[...]))
        a = jnp.sin(fr * (_dot(a.astype(BF16), w2_ref[...]) + b2_ref[...]))
        a3_ref[...] = a.astype(BF16)

    @pl.when(dstep == 0)
    def _():
        t = lax.broadcasted_iota(jnp.int32, (n, 1), 0)
        decay = jnp.exp(-(t.astype(F32) / (n - 1.0)) * dl_ref[...])
        a3 = a3_ref[...]
        hb = jnp.where(t == 0, 0.0, _dot(a3, w3b_ref[...]) * decay)
        hf = _dot(a3, w3f_ref[...]) * decay
        norm = (jnp.sum(jnp.abs(hb), axis=0, keepdims=True)
                + jnp.sum(jnp.abs(hf), axis=0, keepdims=True))
        f_ref[0:n, :] = (hb / norm).astype(BF16)
        f_ref[n:2 * n, :] = (hf / norm).astype(BF16)

    rows = g_ref.shape[1]
    k = lax.broadcasted_iota(jnp.int32, (rows, 1), 0)
    alt = jnp.where(k < 2 * m, 1 - 2 * jnp.bitwise_and(k, 1), 1).astype(F32)
    unit = jnp.where(jnp.logical_or(k < m, k == 2 * m), 1.0, 0.0)

    def transform(idx, row):
        return _dot(mats_ref[idx], f_ref[pl.ds(pl.multiple_of(row, m), m), :])

    def further(t, row):
        tap0 = f_ref[pl.ds(pl.multiple_of(row, m), NYQ_ROWS), :].astype(F32)[0:1, :]
        return alt * (t - unit * tap0)

    @pl.when(dstep == 0)
    def _():
        t_ref[...] = transform(BWD0, (nb - 1) * m)

    d = dstep - (nb - 1)
    back = d <= -1
    new_row = jnp.where(back, (-d - 1) * m, n + d * m)
    old_row = jnp.where(d <= 0, -d * m, n + (d - 1) * m)
    t_new = transform(jnp.where(back, BWD0, FWD0), new_row)
    t_old = t_ref[...]
    a = back.astype(F32)
    b = (d >= 1).astype(F32)
    g = ((1.0 - a) * t_new + a * further(t_new, new_row)
         + (1.0 - b) * t_old + b * further(t_old, old_row))
    g_ref[0] = g.astype(g_ref.dtype)
    t_ref[...] = t_new


def _hy_filter(n, m, w0, b0, w1, b1, w2, b2, w3, freq, mats):
    d = w3.shape[1] // 2
    order = w1.shape[0]
    nb = n // m
    ct = _tile(d, 256)
    emb_pad = 128
    z = _filter_features(n, emb_pad)
    w0p = jnp.zeros((emb_pad, order), F32).at[:w0.shape[0]].set(w0).astype(BF16)
    lo, hi = math.log(HY_TARGET) / HY_SLOW, math.log(HY_TARGET) / HY_FAST
    deltas = jnp.asarray(np.abs(np.linspace(lo, hi, d))[None, :], F32)
    w3b = w3.astype(BF16)
    rows = 2 * m + NYQ_ROWS
    full = lambda shp: pl.BlockSpec(shp, lambda c, s: (0,) * len(shp))
    return pl.pallas_call(
        functools.partial(_hy_filter_kernel, n=n, m=m),
        grid=(d // ct, 2 * nb - 1),
        in_specs=[full((n, emb_pad)), full((emb_pad, order)), full((1, order)),
                  full((order, order)), full((1, order)), full((order, order)), full((1, order)),
                  full((1, order)),
                  pl.BlockSpec((order, ct), lambda c, s: (0, d // ct + c)),
                  pl.BlockSpec((order, ct), lambda c, s: (0, c)),
                  pl.BlockSpec((1, ct), lambda c, s: (0, c)),
                  _resident((2, rows, m), lambda c, s: (0, 0, 0))],
        out_specs=pl.BlockSpec((1, rows, ct), lambda c, s: (s, 0, c)),
        out_shape=jax.ShapeDtypeStruct((2 * nb - 1, rows, d), BF16),
        scratch_shapes=[pltpu.VMEM((n, order), BF16), pltpu.VMEM((2 * n, ct), BF16),
                        pltpu.VMEM((rows, ct), F32)],
        compiler_params=_cparams(2),
        name="hyena_filter",
    )(z, w0p, b0.reshape(1, order), w1.astype(BF16), b1.reshape(1, order), w2.astype(BF16),
      b2.reshape(1, order), freq.reshape(1, order), w3b, w3b, deltas, mats)


def _hyena_in_kernel(x_ref, xp_ref, xn_ref, mod_ref, g_ref, win_ref, cw_ref, cb_ref, x0_ref, v_ref, *,
                     seq_len, row_base, tpb, ct):
    i = pl.program_id(0)
    tm, d = x_ref.shape
    r = _mod_row(i, row_base, tpb)
    xcat = jnp.concatenate([xp_ref[...], x_ref[...], xn_ref[...]], axis=0)
    h = _modnorm(xcat, g_ref[...], _mod_chunk(mod_ref, r, SC1, d), _mod_chunk(mod_ref, r, SH1, d)).astype(BF16)
    rows = tm + 2 * HALO
    inner = slice(HALO, HALO + tm)
    starts, ends, (start_live, end_live) = _seq_edges(i, tm, seq_len)

    def conv(col):
        p = _dot(h, win_ref[:, col:col + ct])
        w = cw_ref[:, col:col + ct]
        dn = _zero_rows(pltpu.roll(p, 1, 0)[inner], starts, start_live)
        up = _zero_rows(pltpu.roll(p, rows - 1, 0)[inner], ends, end_live)
        return dn * w[0:1, :] + p[inner] * w[1:2, :] + up * w[2:3, :] + cb_ref[:, col:col + ct]

    for c in range(0, d, ct):
        x0_ref[:, c:c + ct] = conv(c).astype(BF16)
        v_ref[:, c:c + ct] = (conv(2 * d + c) * conv(d + c)).astype(BF16)


def _hyena_in(x, mod, g, w_in, layer, conv_w, conv_b, *, seq_len, row_base, rows_per_batch):
    n, d = x.shape
    assert seq_len & (seq_len - 1) == 0
    tm = min(1024, n if rows_per_batch is None else rows_per_batch)
    tpb = None if rows_per_batch is None else rows_per_batch // tm
    hb = tm // HALO
    last = n // HALO - 1
    out = pl.BlockSpec((tm, d), lambda i: (i, 0))
    return pl.pallas_call(
        functools.partial(_hyena_in_kernel, seq_len=seq_len, row_base=row_base, tpb=tpb, ct=_tile(d, 256)),
        grid=(n // tm,),
        in_specs=[pl.BlockSpec((tm, d), lambda i: (i, 0)),
                  pl.BlockSpec((HALO, d), lambda i: (jnp.maximum(i * hb - 1, 0), 0)),
                  pl.BlockSpec((HALO, d), lambda i: (jnp.minimum((i + 1) * hb, last), 0)),
                  pl.BlockSpec(mod.shape, lambda i: (0, 0)),
                  pl.BlockSpec((1, d), lambda i: (0, 0)),
                  _resident((None, d, 3 * d), lambda i: (layer, 0, 0)),
                  pl.BlockSpec((3, 3 * d), lambda i: (0, 0)),
                  pl.BlockSpec((1, 3 * d), lambda i: (0, 0))],
        out_specs=[out, out],
        out_shape=[jax.ShapeDtypeStruct((n, d), BF16), jax.ShapeDtypeStruct((n, d), BF16)],
        compiler_params=_cparams(1),
        name="hyena_in",
    )(x, x, x, mod, g.reshape(1, d), w_in, conv_w, conv_b.reshape(1, 3 * d))


def _hy_conv_kernel(x0_ref, v_ref, hb_ref, g_ref, ma_ref, minv_ref, o_ref, vf_ref, *, m, nb):
    i = pl.program_id(2)

    @pl.when(i == 0)
    def _():
        for j in range(nb):
            vf_ref[j] = _dot(ma_ref[...], v_ref[j * m:(j + 1) * m, :])

    ct = o_ref.shape[1]
    acc_r = jnp.zeros((m, ct), F32)
    acc_i = jnp.zeros((m, ct), F32)
    acc_n = jnp.zeros((NYQ_ROWS, ct), F32)
    for j in range(nb):
        g = g_ref[i - j + nb - 1].astype(F32)
        vr = vf_ref[j, 0:m, :]
        vi = vf_ref[j, m:2 * m, :]
        gr = g[0:m, :]
        gi = g[m:2 * m, :]
        acc_r += gr * vr - gi * vi
        acc_i += gr * vi + gi * vr
        acc_n += g[2 * m:, :] * vf_ref[j, 2 * m:, :]
    spec = jnp.concatenate([acc_r, acc_i], axis=0).astype(BF16)
    t = lax.broadcasted_iota(jnp.int32, (m, 1), 0)
    sgn = (1 - 2 * jnp.bitwise_and(t, 1)).astype(F32) * (0.5 / m)
    y = _dot(minv_ref[...], spec) + sgn * acc_n[0:1, :]
    start = pl.multiple_of(i * m, m)
    v1 = v_ref[pl.ds(start, m), :].astype(F32)
    x0 = x0_ref[pl.ds(start, m), :].astype(F32)
    o_ref[...] = ((y + v1 * hb_ref[...]) * x0).astype(o_ref.dtype)


def _hy_conv_short_kernel(x0_ref, v_ref, hb_ref, g_ref, ma_ref, minv_ref, o_ref, *, m, spb):
    g = g_ref[0].astype(F32)
    gr, gi, gn = g[0:m, :], g[m:2 * m, :], g[2 * m:, :]
    t = lax.broadcasted_iota(jnp.int32, (m, 1), 0)
    sgn = (1 - 2 * jnp.bitwise_and(t, 1)).astype(F32) * (0.5 / m)
    for s in range(spb):
        rows = slice(s * m, (s + 1) * m)
        vb = v_ref[rows, :]
        vf = _dot(ma_ref[...], vb)
        vr, vi = vf[0:m, :], vf[m:2 * m, :]
        spec = jnp.concatenate([gr * vr - gi * vi, gr * vi + gi * vr], axis=0).astype(BF16)
        y = _dot(minv_ref[...], spec) + sgn * (gn * vf[2 * m:, :])[0:1, :]
        o_ref[rows, :] = ((y + vb.astype(F32) * hb_ref[...]) * x0_ref[rows, :].astype(F32)).astype(o_ref.dtype)


def _hy_conv(x0, v, hy_bias, g, ma, minv, *, n, m, batch):
    d = hy_bias.shape[0]
    nb = n // m
    ct = _tile(d, 256)
    rows = 2 * m + NYQ_ROWS
    if nb == 1:
        spb = max(s for s in range(1, 9) if batch % s == 0)
        blk = pl.BlockSpec((spb * n, ct), lambda c, b: (b, c))
        return pl.pallas_call(
            functools.partial(_hy_conv_short_kernel, m=m, spb=spb),
            grid=(d // ct, batch // spb),
            in_specs=[blk, blk,
                      pl.BlockSpec((1, ct), lambda c, b: (0, c)),
                      pl.BlockSpec((1, rows, ct), lambda c, b: (0, 0, c)),
                      _resident((rows, m), lambda c, b: (0, 0)),
                      _resident((m, 2 * m), lambda c, b: (0, 0))],
            out_specs=blk,
            out_shape=jax.ShapeDtypeStruct((batch * n, d), BF16),
            compiler_params=_cparams(2),
            name="hyena_conv_short",
        )(x0, v, hy_bias.reshape(1, d), g, ma, minv)
    seq = pl.BlockSpec((n, ct), lambda c, b, i: (b, c))
    return pl.pallas_call(
        functools.partial(_hy_conv_kernel, m=m, nb=nb),
        grid=(d // ct, batch, nb),
        in_specs=[seq, seq,
                  pl.BlockSpec((1, ct), lambda c, b, i: (0, c)),
                  _resident((2 * nb - 1, rows, ct), lambda c, b, i: (0, 0, c)),
                  _resident((rows, m), lambda c, b, i: (0, 0)),
                  _resident((m, 2 * m), lambda c, b, i: (0, 0))],
        out_specs=pl.BlockSpec((m, ct), lambda c, b, i: (b * nb + i, c)),
        out_shape=jax.ShapeDtypeStruct((batch * n, d), BF16),
        scratch_shapes=[pltpu.VMEM((nb, rows, ct), F32)],
        compiler_params=_cparams(3),
        name="hyena_conv",
    )(x0, v, hy_bias.reshape(1, d), g, ma, minv)


def _rope_tables(n):
    axis = ROPE_DIM // 2
    half = axis // 2
    inv = ROPE_THETA ** (-np.arange(0, axis, 2) / axis)
    t = np.arange(n)
    row, col = t // GRID_W, t % GRID_W
    cos = np.ones((n, HEAD_SLOT))
    sin_a = np.zeros((n, HEAD_SLOT))
    sin_b = np.zeros((n, HEAD_SLOT))
    for k, pos in enumerate((row, col)):
        ang = pos[:, None] * inv[None, :]
        base = QK_NOPE + k * axis
        cos[:, base:base + half] = np.cos(ang)
        cos[:, base + half:base + axis] = np.cos(ang)
        sin_a[:, base:base + half] = -np.sin(ang)
        sin_b[:, base + half:base + axis] = np.sin(ang)
    return jnp.asarray(cos.T, F32), jnp.asarray(sin_a.T, F32), jnp.asarray(sin_b.T, F32)


def _expand_keys(ckv_b, kpe_slot, wukv_ref, kng, k_ref, v_ref):
    kw = N_HEADS * HEAD_SLOT
    kv = _dot(ckv_b, wukv_ref[...])
    for h in range(N_HEADS):
        sl = slice(h * HEAD_SLOT, (h + 1) * HEAD_SLOT)
        x = kv[:, sl] + kpe_slot
        scale = lax.rsqrt(jnp.sum(x * x, axis=-1, keepdims=True) * (1.0 / QK_DIM) + EPS)
        k_ref[:, sl] = (x * scale * kng).astype(BF16)
    v_ref[...] = kv[:, kw:].T.astype(BF16)


def _rmsnorm_rows(x, g_col):
    ms = jnp.mean(x * x, axis=0, keepdims=True)
    return x * lax.rsqrt(ms + EPS) * g_col


def _rope_gains(g3, rope):
    if rope is None:
        return g3[:, 0:1], None, None
    cos, sin_a, sin_b = rope
    return g3[:, 0:1] * cos, g3[:, 1:2] * sin_a, g3[:, 2:3] * sin_b


def _partner_term(x, ga, gb):
    half = ROPE_DIM // 4
    return pltpu.roll(x, HEAD_SLOT - half, 0) * ga + pltpu.roll(x, half, 0) * gb


def _slab_norm_rope(x, gc, partner):
    scale = lax.rsqrt(jnp.sum(x * x, axis=0, keepdims=True) * (1.0 / QK_DIM) + EPS)
    y = x * gc
    if partner is not None:
        y = y + partner
    return y * scale


def _mla_prep_kernel(*refs, row_base, tpb, use_rope, emit_latent):
    (x_ref, mod_ref, g_ref, wcat_ref, qg_ref, kvg_ref, wuq_ref, wukv_ref, qng_ref, kng_ref), refs = refs[:10], refs[10:]
    if use_rope:
        (cos_ref, sa_ref, sb_ref), refs = refs[:3], refs[3:]
        rope = (cos_ref[...], sa_ref[...], sb_ref[...])
    else:
        rope = None
    q_ref, k_ref, v_ref = refs[:3]
    i = pl.program_id(0)
    d = x_ref.shape[1]
    r = _mod_row(i, row_base, tpb)
    h = _modnorm(x_ref[...], g_ref[...], _mod_chunk(mod_ref, r, SC1, d), _mod_chunk(mod_ref, r, SH1, d))
    at = _dot(wcat_ref[...], h.T.astype(BF16))
    q_rank = qg_ref.shape[0]
    kv_rank = kvg_ref.shape[0]
    qct = _rmsnorm_rows(at[:q_rank], qg_ref[...]).astype(BF16)
    ckvt = _rmsnorm_rows(at[q_rank:q_rank + kv_rank], kvg_ref[...])
    kpet = at[q_rank + kv_rank:]
    if emit_latent:
        refs[3][...] = ckvt.T
        refs[4][...] = kpet.T
    qt = _dot(wuq_ref[...], qct)
    qc, qa, qb = _rope_gains(qng_ref[...] * (QK_DIM ** -0.5 * math.log2(math.e)), rope)
    for hd in range(N_HEADS):
        sl = slice(hd * HEAD_SLOT, (hd + 1) * HEAD_SLOT)
        x = qt[sl]
        q_ref[sl, :] = _slab_norm_rope(x, qc, _partner_term(x, qa, qb) if use_rope else None).astype(BF16)
    kw = N_HEADS * HEAD_SLOT
    kvt = _dot(wukv_ref[...], ckvt.astype(BF16))
    v_ref[...] = kvt[kw:].astype(BF16)
    kc, ka, kb = _rope_gains(kng_ref[...], rope)
    partner = _partner_term(kpet, ka, kb) if use_rope else None
    for hd in range(N_HEADS):
        sl = slice(hd * HEAD_SLOT, (hd + 1) * HEAD_SLOT)
        k_ref[:, sl] = _slab_norm_rope(kvt[sl] + kpet, kc, partner).T.astype(BF16)


def _mla_prep(x, mod, g, wcat_t, qg, kvg, wuq_t, wukv_t, qng3, kng3, rope_t, *, row_base, rows_per_batch,
              emit_latent):
    n, d = x.shape
    seq = n if rows_per_batch is None else rows_per_batch
    tm = min(512, seq)
    tpb = None if rows_per_batch is None else rows_per_batch // tm
    kw = N_HEADS * HEAD_SLOT
    vw = N_HEADS * V_DIM
    q_rank, kv_rank = qg.shape[0], kvg.shape[0]
    full = lambda a: pl.BlockSpec(a.shape, lambda i: (0,) * a.ndim)
    args = [x, mod, g.reshape(1, d), wcat_t, qg.reshape(-1, 1), kvg.reshape(-1, 1), wuq_t, wukv_t, qng3, kng3]
    in_specs = [pl.BlockSpec((tm, d), lambda i: (i, 0))] + [full(a) for a in args[1:]]
    if rope_t is not None:
        pt = rows_per_batch // tm
        args += list(rope_t)
        in_specs += [pl.BlockSpec((HEAD_SLOT, tm), lambda i: (0, i % pt))] * 3
    row = lambda w: pl.BlockSpec((tm, w), lambda i: (i, 0))
    col = lambda w: pl.BlockSpec((w, tm), lambda i: (0, i))
    out_specs = [col(kw), row(kw), col(vw)]
    out_shape = [jax.ShapeDtypeStruct((kw, n), BF16), jax.ShapeDtypeStruct((n, kw), BF16),
                 jax.ShapeDtypeStruct((vw, n), BF16)]
    if emit_latent:
        out_specs += [row(kv_rank), row(HEAD_SLOT)]
        out_shape += [jax.ShapeDtypeStruct((n, kv_rank), F32), jax.ShapeDtypeStruct((n, HEAD_SLOT), F32)]
    return pl.pallas_call(
        functools.partial(_mla_prep_kernel, row_base=row_base, tpb=tpb, use_rope=rope_t is not None,
                          emit_latent=emit_latent),
        grid=(n // tm,),
        in_specs=in_specs, out_specs=out_specs, out_shape=out_shape,
        compiler_params=_cparams(1),
        name="mla_prep",
    )(*args)


def _mla_cache_kernel(ckv_ref, kpe_ref, wukv_ref, kng_ref, k_ref, v_ref):
    _expand_keys(ckv_ref[...].astype(BF16), kpe_ref[...], wukv_ref, kng_ref[...], k_ref, v_ref)


def _mla_cache(ckv, kpe_slot, wukv, kng):
    n = ckv.shape[0]
    tm = min(512, n)
    kw = N_HEADS * HEAD_SLOT
    vw = N_HEADS * V_DIM
    row = lambda w: pl.BlockSpec((tm, w), lambda i: (i, 0))
    return pl.pallas_call(
        _mla_cache_kernel,
        grid=(n // tm,),
        in_specs=[row(ckv.shape[1]), row(HEAD_SLOT), pl.BlockSpec(wukv.shape, lambda i: (0, 0)),
                  pl.BlockSpec(kng.shape, lambda i: (0, 0))],
        out_specs=[row(kw), pl.BlockSpec((vw, tm), lambda i: (0, i))],
        out_shape=[jax.ShapeDtypeStruct((n, kw), BF16), jax.ShapeDtypeStruct((vw, n), BF16)],
        compiler_params=_cparams(1),
        name="mla_cache_keys",
    )(ckv, kpe_slot, wukv, kng)


KEY_CHUNK = 256


def _attn_kernel(*refs, has_cache, heads):
    if has_cache:
        q_ref, k_ref, v_ref, kc_ref, vc_ref, o_ref = refs
    else:
        q_ref, k_ref, v_ref, o_ref = refs
    n = k_ref.shape[0]
    outs = []
    for hh in range(heads):
        sl = slice(hh * HEAD_SLOT, (hh + 1) * HEAD_SLOT)
        vs = slice(hh * V_DIM, (hh + 1) * V_DIM)
        q = q_ref[sl, :]
        pieces = [(k_ref, v_ref, c, min(KEY_CHUNK, n - c)) for c in range(0, n, KEY_CHUNK)]
        if has_cache:
            pieces.append((kc_ref, vc_ref, 0, kc_ref.shape[0]))
        sts = [_dot(kr[c:c + w, sl], q) for kr, _, c, w in pieces]
        mx = functools.reduce(jnp.maximum, [jnp.max(st, axis=0, keepdims=True) for st in sts])
        pts = [jnp.exp2(st - mx) for st in sts]
        l = functools.reduce(jnp.add, [jnp.sum(pt, axis=0, keepdims=True) for pt in pts])
        ot = functools.reduce(jnp.add, [_dot(vr[vs, c:c + w], pt.astype(BF16))
                                        for (_, vr, c, w), pt in zip(pieces, pts)])
        outs.append(ot / l)
    o_ref[...] = jnp.concatenate(outs, axis=0).T.astype(o_ref.dtype)


def _attention(q, k, v, kc, vc, *, batch, n, past):
    has_cache = kc is not None
    tq = min(512, n)
    heads = 2 if n > tq else N_HEADS
    nq = n // tq
    groups = N_HEADS // heads
    in_specs = [pl.BlockSpec((heads * HEAD_SLOT, tq), lambda b, j, i: (j, b * nq + i)),
                pl.BlockSpec((n, heads * HEAD_SLOT), lambda b, j, i: (b, j)),
                pl.BlockSpec((heads * V_DIM, n), lambda b, j, i: (j, b))]
    args = [q, k, v]
    if has_cache:
        in_specs += [pl.BlockSpec((past, heads * HEAD_SLOT), lambda b, j, i: (b, j)),
                     pl.BlockSpec((heads * V_DIM, past), lambda b, j, i: (j, b))]
        args += [kc, vc]
    return pl.pallas_call(
        functools.partial(_attn_kernel, has_cache=has_cache, heads=heads),
        grid=(batch, groups, nq),
        in_specs=in_specs,
        out_specs=pl.BlockSpec((tq, heads * V_DIM), lambda b, j, i: (b * nq + i, j)),
        out_shape=jax.ShapeDtypeStruct((batch * n, N_HEADS * V_DIM), BF16),
        compiler_params=_cparams(3),
        name="attention",
    )(*args)


def _pad_heads(w, width):
    k = w.shape[0]
    w = w.reshape(k, N_HEADS, width)
    return jnp.pad(w, ((0, 0), (0, 0), (0, HEAD_SLOT - width))).reshape(k, N_HEADS * HEAD_SLOT)


def _slot_gain(g):
    return jnp.pad(g, (0, HEAD_SLOT - QK_DIM)).reshape(1, HEAD_SLOT)


def _gain_columns(g_row):
    g = g_row.reshape(HEAD_SLOT)
    half = ROPE_DIM // 4
    return jnp.stack([g, jnp.roll(g, -half), jnp.roll(g, half)], axis=1)


def kernel(x_prompt, x_sample, cache_ckv, cache_kpe, c, c_ctx, norm1_g, norm2_g, mod_w, mod_b, mlp_w1, mlp_w2, sc_w_in, sc_conv_w, sc_conv_b, sc_w_out, mla_w_dq, mla_q_norm_g, mla_w_uq, mla_w_dkv, mla_kv_norm_g, mla_w_ukv, mla_qn_g, mla_kn_g, mla_w_o, hy_w_in, hy_conv_w, hy_conv_b, hy_f_w0, hy_f_b0, hy_f_w1, hy_f_b1, hy_f_w2, hy_f_b2, hy_f_w3, hy_sin_freq, hy_bias, hy_w_out):
    batch, seq, d = x_prompt.shape
    dec_batch, dec_seq, _ = x_sample.shape
    depth = mod_w.shape[0]
    n_mla = mla_w_dq.shape[0]
    past = cache_ckv.shape[2]
    kv_rank = mla_kv_norm_g.shape[1]
    assert 1 + dec_batch <= MOD_ROWS

    cond = jnp.zeros((MOD_ROWS, d), F32).at[0].set(c_ctx).at[1:1 + dec_batch].set(c)
    mods = _modulation(cond, mod_w, mod_b)

    w1 = mlp_w1.astype(BF16)
    w2 = mlp_w2.astype(BF16)
    sc_in = sc_w_in.astype(BF16)
    sc_out = sc_w_out.astype(BF16)
    hy_in = hy_w_in.astype(BF16)
    hy_out = hy_w_out.astype(BF16)
    mla_o = mla_w_o.astype(BF16)

    streams = [
        dict(x=x_prompt.reshape(batch * seq, d), row_base=0, rpb=None, seq=seq, nseq=batch, ctx=True),
        dict(x=x_sample.reshape(dec_batch * dec_seq, d), row_base=1, rpb=dec_seq, seq=dec_seq,
             nseq=dec_batch, ctx=False),
    ]
    hy_consts = {}
    new_ckv, new_kpe = [], []

    for i in range(depth):
        kind, j = i % 3, i // 3
        mod = mods[i]
        for st in streams:
            x = st["x"]
            rb, rpb, n = st["row_base"], st["rpb"], st["seq"]
            tail = None
            if kind == 0:
                x = _sconv(x, mod, norm1_g[i], sc_in, sc_conv_w[j], sc_conv_b[j], sc_out, j, seq_len=n,
                           row_base=rb, rows_per_batch=rpb)
            elif kind == 1:
                pe = mla_w_dkv[j][:, kv_rank:]
                pe_slot = jnp.pad(pe, ((0, 0), (QK_NOPE, HEAD_SLOT - QK_DIM)))
                wcat = jnp.concatenate([mla_w_dq[j], mla_w_dkv[j][:, :kv_rank], pe_slot], axis=1).astype(BF16)
                wuq = _pad_heads(mla_w_uq[j], QK_DIM).astype(BF16)
                ukv = mla_w_ukv[j].reshape(kv_rank, N_HEADS, QK_NOPE + V_DIM)
                wukv = jnp.concatenate(
                    [_pad_heads(ukv[:, :, :QK_NOPE].reshape(kv_rank, -1), QK_NOPE),
                     ukv[:, :, QK_NOPE:].reshape(kv_rank, -1)], axis=1).astype(BF16)
                qng, kng = _slot_gain(mla_qn_g[j]), _slot_gain(mla_kn_g[j])
                rope = None if st["ctx"] else _rope_tables(n)
                outs = _mla_prep(x, mod, norm1_g[i], wcat.T, mla_q_norm_g[j], mla_kv_norm_g[j], wuq.T, wukv.T,
                                 _gain_columns(qng), _gain_columns(kng), rope, row_base=rb,
                                 rows_per_batch=rpb, emit_latent=st["ctx"])
                q, k, v = outs[:3]
                if st["ctx"]:
                    new_ckv.append(outs[3].reshape(batch, seq, kv_rank))
                    new_kpe.append(outs[4][:, QK_NOPE:QK_DIM].reshape(batch, seq, ROPE_DIM))
                    kc = vc = None
                else:
                    ck = cache_ckv[:, j].reshape(dec_batch * past, kv_rank)
                    cp = jnp.pad(cache_kpe[:, j].reshape(dec_batch * past, ROPE_DIM),
                                 ((0, 0), (QK_NOPE, HEAD_SLOT - QK_DIM)))
                    kc, vc = _mla_cache(ck, cp, wukv, kng)
                o = _attention(q, k, v, kc, vc, batch=st["nseq"], n=n, past=past)
                tail = (o, mla_o, j)
            else:
                m = min(HY_BLOCK, n)
                if m not in hy_consts:
                    hy_consts[m] = _dft_constants(m)
                mats, minv = hy_consts[m]
                g = _hy_filter(n, m, hy_f_w0[j], hy_f_b0[j], hy_f_w1[j], hy_f_b1[j], hy_f_w2[j],
                               hy_f_b2[j], hy_f_w3[j], hy_sin_freq[j], mats)
                x0, v1 = _hyena_in(x, mod, norm1_g[i], hy_in, j, hy_conv_w[j], hy_conv_b[j], seq_len=n,
                                   row_base=rb, rows_per_batch=rpb)
                z = _hy_conv(x0, v1, hy_bias[j], g, mats[FWD0], minv, n=n, m=m, batch=st["nseq"])
                tail = (z, hy_out, j)
            x = _mlp(x, mod, norm2_g[i], w1, w2, i, row_base=rb, rows_per_batch=rpb, tail=tail)
            st["x"] = x

    y_prompt = streams[0]["x"].reshape(batch, seq, d)
    y_sample = streams[1]["x"].reshape(dec_batch, dec_seq, d)
    return (y_prompt, y_sample, jnp.stack(new_ckv, axis=1), jnp.stack(new_kpe, axis=1))
```

```python
import functools
import math

import numpy as np
import jax
import jax.numpy as jnp
from jax import lax
from jax.experimental import pallas as pl
from jax.experimental.pallas import tpu as pltpu

F32 = jnp.float32
BF16 = jnp.bfloat16

EPS = 1e-6
MOD_CHUNKS = 6
N_HEADS = 16
QK_NOPE = 64
ROPE_DIM = 32
QK_DIM = QK_NOPE + ROPE_DIM
V_DIM = 64
HEAD_SLOT = 128
GRID_W = 64
ROPE_THETA = 10000.0
HY_TARGET = 1e-2
HY_FAST = 0.3
HY_SLOW = 1.5
HY_BLOCK = 1024
NYQ_ROWS = 16
MOD_ROWS = 8

V7X_VMEM_BYTES = 64 * 1024 * 1024
VMEM_LIMIT = V7X_VMEM_BYTES - 8 * 1024 * 1024

SH1, SC1, G1, SH2, SC2, G2 = range(6)


def _tile(n, pref, align=128):
    if n <= pref:
        return n
    t = pref - pref % align
    while n % t:
        t -= align
    return t


def _cparams(n_axes):
    return pltpu.CompilerParams(dimension_semantics=("arbitrary",) * n_axes,
                                vmem_limit_bytes=VMEM_LIMIT)


def _mod_chunk(mod_ref, r, k, d):
    return mod_ref[pl.ds(r, 1), k * d:(k + 1) * d]


def _mod_row(i, row_base, tiles_per_batch):
    if tiles_per_batch is None:
        return row_base
    return row_base + i // tiles_per_batch


def _modnorm(x, g, sc, sh):
    ms = jnp.mean(x * x, axis=-1, keepdims=True)
    return x * lax.rsqrt(ms + EPS) * (g * (1.0 + sc)) + sh


def _dot(a, b):
    return jnp.dot(a, b, preferred_element_type=F32)


def _mod_kernel(c_ref, w_ref, b_ref, o_ref):
    c = c_ref[...]
    s = (c * jax.nn.sigmoid(c)).astype(BF16)
    o_ref[0] = _dot(s, w_ref[0].astype(BF16)) + b_ref[0]


def _modulation(cond, mod_w, mod_b):
    depth, d, n6 = mod_w.shape
    tn = _tile(n6, 1536)
    return pl.pallas_call(
        _mod_kernel,
        grid=(depth, n6 // tn),
        in_specs=[pl.BlockSpec((MOD_ROWS, d), lambda l, j: (0, 0)),
                  pl.BlockSpec((1, d, tn), lambda l, j: (l, 0, j)),
                  pl.BlockSpec((1, 1, tn), lambda l, j: (l, 0, j))],
        out_specs=pl.BlockSpec((1, MOD_ROWS, tn), lambda l, j: (l, 0, j)),
        out_shape=jax.ShapeDtypeStruct((depth, MOD_ROWS, n6), F32),
        compiler_params=_cparams(2),
        name="modulation",
    )(cond, mod_w, mod_b.reshape(depth, 1, n6))


def _resident(shape, index_map):
    return pl.BlockSpec(shape, index_map, pipeline_mode=pl.Buffered(1))


def _mlp_kernel(*refs, row_base, tpb, tf, mixer_tail):
    if mixer_tail:
        z_ref, wo_ref, refs = refs[0], refs[1], refs[2:]
    x_ref, mod_ref, g_ref, w1_ref, w2_ref, o_ref = refs
    d = x_ref.shape[1]
    r = _mod_row(pl.program_id(0), row_base, tpb)
    x = x_ref[...]
    if mixer_tail:
        x = x + _mod_chunk(mod_ref, r, G1, d) * _dot(z_ref[...], wo_ref[...])
    h = _modnorm(x, g_ref[...], _mod_chunk(mod_ref, r, SC2, d), _mod_chunk(mod_ref, r, SH2, d)).astype(BF16)
    acc = None
    for c in range(0, w1_ref.shape[1], tf):
        a = jnp.maximum(_dot(h, w1_ref[:, c:c + tf]), 0.0)
        y = _dot((a * a).astype(BF16), w2_ref[c:c + tf, :])
        acc = y if acc is None else acc + y
    o_ref[...] = x + _mod_chunk(mod_ref, r, G2, d) * acc


def _mlp(x, mod, g, w1, w2, layer, *, row_base, rows_per_batch, tail=None):
    n, d = x.shape
    dff = w1.shape[2]
    tm = min(1024, n if rows_per_batch is None else rows_per_batch)
    tpb = None if rows_per_batch is None else rows_per_batch // tm
    args = [x, mod, g.reshape(1, d), w1, w2]
    in_specs = [pl.BlockSpec((tm, d), lambda i: (i, 0)),
                pl.BlockSpec(mod.shape, lambda i: (0, 0)),
                pl.BlockSpec((1, d), lambda i: (0, 0)),
                _resident((None, d, dff), lambda i: (layer, 0, 0)),
                _resident((None, dff, d), lambda i: (layer, 0, 0))]
    if tail is not None:
        z, w_out, tail_layer = tail
        k = z.shape[1]
        args = [z, w_out] + args
        in_specs = [pl.BlockSpec((tm, k), lambda i: (i, 0)),
                    _resident((None, k, d), lambda i: (tail_layer, 0, 0))] + in_specs
    return pl.pallas_call(
        functools.partial(_mlp_kernel, row_base=row_base, tpb=tpb, tf=_tile(dff, 1024),
                          mixer_tail=tail is not None),
        grid=(n // tm,),
        in_specs=in_specs,
        out_specs=pl.BlockSpec((tm, d), lambda i: (i, 0)),
        out_shape=jax.ShapeDtypeStruct((n, d), F32),
        compiler_params=_cparams(1),
        name="mlp",
    )(*args)


HALO = 16
SUBLANES = 8


def _seq_edges(i, tm, seq_len):
    if seq_len <= tm:
        return list(range(0, tm, seq_len)), list(range(seq_len - 1, tm, seq_len)), (None, None)
    mask = seq_len - 1
    return [0], [tm - 1], (jnp.bitwise_and(i * tm, mask) == 0, jnp.bitwise_and((i + 1) * tm, mask) == 0)


def _zero_rows(x, rows, live):
    sub = lax.broadcasted_iota(jnp.int32, (SUBLANES, 1), 0)
    pieces, done = [], 0
    for grp in sorted({r // SUBLANES for r in rows}):
        lo = grp * SUBLANES
        if lo > done:
            pieces.append(x[done:lo])
        hit = None
        for r in rows:
            if r // SUBLANES == grp:
                target = r % SUBLANES if live is None else jnp.where(live, r % SUBLANES, -1)
                hit = (sub == target) if hit is None else jnp.logical_or(hit, sub == target)
        pieces.append(jnp.where(hit, 0.0, x[lo:lo + SUBLANES]))
        done = lo + SUBLANES
    if done < x.shape[0]:
        pieces.append(x[done:])
    return jnp.concatenate(pieces, axis=0)


def _sconv_kernel(x_ref, xp_ref, xn_ref, mod_ref, g_ref, win_ref, cw_ref, cb_ref, wout_ref, o_ref, *,
                  seq_len, row_base, tpb):
    i = pl.program_id(0)
    tm, d = x_ref.shape
    r = _mod_row(i, row_base, tpb)
    x = x_ref[...]
    xcat = jnp.concatenate([xp_ref[...], x, xn_ref[...]], axis=0)
    h = _modnorm(xcat, g_ref[...], _mod_chunk(mod_ref, r, SC1, d), _mod_chunk(mod_ref, r, SH1, d)).astype(BF16)
    cx = _dot(h, win_ref[:, d:2 * d]) * _dot(h, win_ref[:, 2 * d:3 * d])
    rows = tm + 2 * HALO
    inner = slice(HALO, HALO + tm)
    starts, ends, (start_live, end_live) = _seq_edges(i, tm, seq_len)
    dn = _zero_rows(pltpu.roll(cx, 1, 0)[inner], starts, start_live)
    up = _zero_rows(pltpu.roll(cx, rows - 1, 0)[inner], ends, end_live)
    cw = cw_ref[...]
    u = dn * cw[0:1, :] + cx[inner] * cw[1:2, :] + up * cw[2:3, :] + cb_ref[...]
    z = (_dot(h[inner], win_ref[:, 0:d]) * u).astype(BF16)
    o_ref[...] = x + _mod_chunk(mod_ref, r, G1, d) * _dot(z, wout_ref[...])


def _sconv(x, mod, g, w_in, conv_w, conv_b, w_out, layer, *, seq_len, row_base, rows_per_batch):
    n, d = x.shape
    assert seq_len & (seq_len - 1) == 0
    tm = min(1024, n if rows_per_batch is None else rows_per_batch)
    tpb = None if rows_per_batch is None else rows_per_batch // tm
    hb = tm // HALO
    last = n // HALO - 1
    return pl.pallas_call(
        functools.partial(_sconv_kernel, seq_len=seq_len, row_base=row_base, tpb=tpb),
        grid=(n // tm,),
        in_specs=[pl.BlockSpec((tm, d), lambda i: (i, 0)),
                  pl.BlockSpec((HALO, d), lambda i: (jnp.maximum(i * hb - 1, 0), 0)),
                  pl.BlockSpec((HALO, d), lambda i: (jnp.minimum((i + 1) * hb, last), 0)),
                  pl.BlockSpec(mod.shape, lambda i: (0, 0)),
                  pl.BlockSpec((1, d), lambda i: (0, 0)),
                  _resident((None, d, 3 * d), lambda i: (layer, 0, 0)),
                  pl.BlockSpec((3, d), lambda i: (0, 0)),
                  pl.BlockSpec((1, d), lambda i: (0, 0)),
                  _resident((None, d, d), lambda i: (layer, 0, 0))],
        out_specs=pl.BlockSpec((tm, d), lambda i: (i, 0)),
        out_shape=jax.ShapeDtypeStruct((n, d), F32),
        compiler_params=_cparams(1),
        name="sconv",
    )(x, x, x, mod, g.reshape(1, d), w_in, conv_w, conv_b.reshape(1, d), w_out)


def _dft_constants(m):
    k = np.arange(m)[:, None]
    s = np.arange(m)[None, :]
    ang = np.pi * ((k * s) % (2 * m)) / m
    rows = 2 * m + NYQ_ROWS
    alt_k = (-1.0) ** np.arange(m)[:, None]
    alt_s = (-1.0) ** np.arange(m)

    def mat(re, im, drop_first):
        out = np.zeros((rows, m))
        out[:m], out[m:2 * m], out[2 * m] = re, im, alt_s
        if drop_first:
            out[:, 0] = 0.0
        return out

    cos, sin = np.cos(ang), np.sin(ang)
    mats = np.stack([mat(cos, -sin, False), mat(alt_k * cos, -alt_k * sin, True),
                     mat(cos, sin, False), mat(alt_k * cos, alt_k * sin, True)])
    ck = np.full((1, m), 2.0)
    ck[0, 0] = 1.0
    minv = np.concatenate([ck * cos.T, -ck * sin.T], axis=1) / (2 * m)
    return jnp.asarray(mats, F32).astype(BF16), jnp.asarray(minv, F32).astype(BF16)


FWD0, FWD1, BWD0, BWD1 = range(4)


def _filter_features(n, n_emb_pad):
    bands = (33 - 1) // 2
    t01 = np.linspace(0.0, 1.0, n)[:, None]
    w = 2.0 * np.pi * np.arange(n)[:, None] / n
    fb = np.linspace(1e-4, bands - 1, bands)[None, :]
    z = np.concatenate([t01, np.cos(fb * w), -np.sin(fb * w)], axis=-1)
    out = np.zeros((n, n_emb_pad))
    out[:, :z.shape[1]] = z
    return jnp.asarray(out, F32)


def _hy_filter_kernel(z_ref, w0_ref, b0_ref, w1_ref, b1_ref, w2_ref, b2_ref, fr_ref, w3b_ref, w3f_ref,
                      dl_ref, mats_ref, g_ref, a3_ref, f_ref, *, n, m):
    c = pl.program_id(0)
    dstep = pl.program_id(1)
    nb = n // m

    @pl.when(jnp.logical_and(c == 0, dstep == 0))
    def _():
        fr = fr_ref[...]
        a = jnp.sin(fr * (_dot(z_ref[...].astype(BF16), w0_ref[...]) + b0_ref[...]))
        a = jnp.sin(fr * (_dot(a.astype(BF16), w1_ref[...]) + b1_ref[...]))
        a = jnp.sin(fr * (_dot(a.astype(BF16), w2_ref[...]) + b2_ref[...]))
        a3_ref[...] = a.astype(BF16)

    @pl.when(dstep == 0)
    def _():
        t = lax.broadcasted_iota(jnp.int32, (n, 1), 0)
        decay = jnp.exp(-(t.astype(F32) / (n - 1.0)) * dl_ref[...])
        a3 = a3_ref[...]
        hb = jnp.where(t == 0, 0.0, _dot(a3, w3b_ref[...]) * decay)
        hf = _dot(a3, w3f_ref[...]) * decay
        norm = (jnp.sum(jnp.abs(hb), axis=0, keepdims=True)
                + jnp.sum(jnp.abs(hf), axis=0, keepdims=True))
        f_ref[0:n, :] = (hb / norm).astype(BF16)
        f_ref[n:2 * n, :] = (hf / norm).astype(BF16)

    d = dstep - (nb - 1)
    ia = jnp.where(d >= 0, FWD0, BWD0)
    ra = jnp.where(d >= 0, n + d * m, -d * m)
    ib = jnp.where(d >= 1, FWD1, jnp.where(d == 0, BWD0, BWD1))
    rb = jnp.where(d >= 1, n + (d - 1) * m, jnp.where(d == 0, 0, (-d - 1) * m))
    ra = pl.multiple_of(ra, m)
    rb = pl.multiple_of(rb, m)
    g = _dot(mats_ref[ia], f_ref[pl.ds(ra, m), :]) + _dot(mats_ref[ib], f_ref[pl.ds(rb, m), :])
    g_ref[0] = g.astype(g_ref.dtype)


def _hy_filter(n, m, w0, b0, w1, b1, w2, b2, w3, freq, mats):
    d = w3.shape[1] // 2
    order = w1.shape[0]
    nb = n // m
    ct = _tile(d, 256)
    emb_pad = 128
    z = _filter_features(n, emb_pad)
    w0p = jnp.zeros((emb_pad, order), F32).at[:w0.shape[0]].set(w0).astype(BF16)
    lo, hi = math.log(HY_TARGET) / HY_SLOW, math.log(HY_TARGET) / HY_FAST
    deltas = jnp.asarray(np.abs(np.linspace(lo, hi, d))[None, :], F32)
    w3b = w3.astype(BF16)
    rows = 2 * m + NYQ_ROWS
    full = lambda shp: pl.BlockSpec(shp, lambda c, s: (0,) * len(shp))
    return pl.pallas_call(
        functools.partial(_hy_filter_kernel, n=n, m=m),
        grid=(d // ct, 2 * nb - 1),
        in_specs=[full((n, emb_pad)), full((emb_pad, order)), full((1, order)),
                  full((order, order)), full((1, order)), full((order, order)), full((1, order)),
                  full((1, order)),
                  pl.BlockSpec((order, ct), lambda c, s: (0, d // ct + c)),
                  pl.BlockSpec((order, ct), lambda c, s: (0, c)),
                  pl.BlockSpec((1, ct), lambda c, s: (0, c)),
                  _resident((4, rows, m), lambda c, s: (0, 0, 0))],
        out_specs=pl.BlockSpec((1, rows, ct), lambda c, s: (s, 0, c)),
        out_shape=jax.ShapeDtypeStruct((2 * nb - 1, rows, d), BF16),
        scratch_shapes=[pltpu.VMEM((n, order), BF16), pltpu.VMEM((2 * n, ct), BF16)],
        compiler_params=_cparams(2),
        name="hyena_filter",
    )(z, w0p, b0.reshape(1, order), w1.astype(BF16), b1.reshape(1, order), w2.astype(BF16),
      b2.reshape(1, order), freq.reshape(1, order), w3b, w3b, deltas, mats)


def _hyena_in_kernel(x_ref, xp_ref, xn_ref, mod_ref, g_ref, win_ref, cw_ref, cb_ref, x0_ref, v_ref, *,
                     seq_len, row_base, tpb, ct):
    i = pl.program_id(0)
    tm, d = x_ref.shape
    r = _mod_row(i, row_base, tpb)
    xcat = jnp.concatenate([xp_ref[...], x_ref[...], xn_ref[...]], axis=0)
    h = _modnorm(xcat, g_ref[...], _mod_chunk(mod_ref, r, SC1, d), _mod_chunk(mod_ref, r, SH1, d)).astype(BF16)
    rows = tm + 2 * HALO
    inner = slice(HALO, HALO + tm)
    starts, ends, (start_live, end_live) = _seq_edges(i, tm, seq_len)

    def conv(col):
        p = _dot(h, win_ref[:, col:col + ct])
        w = cw_ref[:, col:col + ct]
        dn = _zero_rows(pltpu.roll(p, 1, 0)[inner], starts, start_live)
        up = _zero_rows(pltpu.roll(p, rows - 1, 0)[inner], ends, end_live)
        return dn * w[0:1, :] + p[inner] * w[1:2, :] + up * w[2:3, :] + cb_ref[:, col:col + ct]

    for c in range(0, d, ct):
        x0_ref[:, c:c + ct] = conv(c).astype(BF16)
        v_ref[:, c:c + ct] = (conv(2 * d + c) * conv(d + c)).astype(BF16)


def _hyena_in(x, mod, g, w_in, layer, conv_w, conv_b, *, seq_len, row_base, rows_per_batch):
    n, d = x.shape
    assert seq_len & (seq_len - 1) == 0
    tm = min(1024, n if rows_per_batch is None else rows_per_batch)
    tpb = None if rows_per_batch is None else rows_per_batch // tm
    hb = tm // HALO
    last = n // HALO - 1
    out = pl.BlockSpec((tm, d), lambda i: (i, 0))
    return pl.pallas_call(
        functools.partial(_hyena_in_kernel, seq_len=seq_len, row_base=row_base, tpb=tpb, ct=_tile(d, 256)),
        grid=(n // tm,),
        in_specs=[pl.BlockSpec((tm, d), lambda i: (i, 0)),
                  pl.BlockSpec((HALO, d), lambda i: (jnp.maximum(i * hb - 1, 0), 0)),
                  pl.BlockSpec((HALO, d), lambda i: (jnp.minimum((i + 1) * hb, last), 0)),
                  pl.BlockSpec(mod.shape, lambda i: (0, 0)),
                  pl.BlockSpec((1, d), lambda i: (0, 0)),
                  _resident((None, d, 3 * d), lambda i: (layer, 0, 0)),
                  pl.BlockSpec((3, 3 * d), lambda i: (0, 0)),
                  pl.BlockSpec((1, 3 * d), lambda i: (0, 0))],
        out_specs=[out, out],
        out_shape=[jax.ShapeDtypeStruct((n, d), BF16), jax.ShapeDtypeStruct((n, d), BF16)],
        compiler_params=_cparams(1),
        name="hyena_in",
    )(x, x, x, mod, g.reshape(1, d), w_in, conv_w, conv_b.reshape(1, 3 * d))


def _hy_conv_kernel(x0_ref, v_ref, hb_ref, g_ref, ma_ref, minv_ref, o_ref, *, m, nb):
    ct = o_ref.shape[1]
    vf = [_dot(ma_ref[...], v_ref[j * m:(j + 1) * m, :]) for j in range(nb)]
    t = lax.broadcasted_iota(jnp.int32, (m, 1), 0)
    sgn = (1 - 2 * jnp.bitwise_and(t, 1)).astype(F32) * (0.5 / m)
    for i in range(nb):
        acc_r = jnp.zeros((m, ct), F32)
        acc_i = jnp.zeros((m, ct), F32)
        acc_n = jnp.zeros((NYQ_ROWS, ct), F32)
        for j in range(nb):
            g = g_ref[i - j + nb - 1].astype(F32)
            vr, vi = vf[j][0:m, :], vf[j][m:2 * m, :]
            gr, gi = g[0:m, :], g[m:2 * m, :]
            acc_r += gr * vr - gi * vi
            acc_i += gr * vi + gi * vr
            acc_n += g[2 * m:, :] * vf[j][2 * m:, :]
        spec = jnp.concatenate([acc_r, acc_i], axis=0).astype(BF16)
        y = _dot(minv_ref[...], spec) + sgn * acc_n[0:1, :]
        rows = slice(i * m, (i + 1) * m)
        v1 = v_ref[rows, :].astype(F32)
        o_ref[rows, :] = ((y + v1 * hb_ref[...]) * x0_ref[rows, :].astype(F32)).astype(o_ref.dtype)


def _hy_conv_short_kernel(x0_ref, v_ref, hb_ref, g_ref, ma_ref, minv_ref, o_ref, *, m, spb):
    g = g_ref[0].astype(F32)
    gr, gi, gn = g[0:m, :], g[m:2 * m, :], g[2 * m:, :]
    t = lax.broadcasted_iota(jnp.int32, (m, 1), 0)
    sgn = (1 - 2 * jnp.bitwise_and(t, 1)).astype(F32) * (0.5 / m)
    for s in range(spb):
        rows = slice(s * m, (s + 1) * m)
        vb = v_ref[rows, :]
        vf = _dot(ma_ref[...], vb)
        vr, vi = vf[0:m, :], vf[m:2 * m, :]
        spec = jnp.concatenate([gr * vr - gi * vi, gr * vi + gi * vr], axis=0).astype(BF16)
        y = _dot(minv_ref[...], spec) + sgn * (gn * vf[2 * m:, :])[0:1, :]
        o_ref[rows, :] = ((y + vb.astype(F32) * hb_ref[...]) * x0_ref[rows, :].astype(F32)).astype(o_ref.dtype)


def _hy_conv(x0, v, hy_bias, g, ma, minv, *, n, m, batch):
    d = hy_bias.shape[0]
    nb = n // m
    ct = _tile(d, 256)
    rows = 2 * m + NYQ_ROWS
    if nb == 1:
        spb = max(s for s in range(1, 9) if batch % s == 0)
        blk = pl.BlockSpec((spb * n, ct), lambda c, b: (b, c))
        return pl.pallas_call(
            functools.partial(_hy_conv_short_kernel, m=m, spb=spb),
            grid=(d // ct, batch // spb),
            in_specs=[blk, blk,
                      pl.BlockSpec((1, ct), lambda c, b: (0, c)),
                      pl.BlockSpec((1, rows, ct), lambda c, b: (0, 0, c)),
                      _resident((rows, m), lambda c, b: (0, 0)),
                      _resident((m, 2 * m), lambda c, b: (0, 0))],
            out_specs=blk,
            out_shape=jax.ShapeDtypeStruct((batch * n, d), BF16),
            compiler_params=_cparams(2),
            name="hyena_conv_short",
        )(x0, v, hy_bias.reshape(1, d), g, ma, minv)
    seq = pl.BlockSpec((n, ct), lambda c, b: (b, c))
    return pl.pallas_call(
        functools.partial(_hy_conv_kernel, m=m, nb=nb),
        grid=(d // ct, batch),
        in_specs=[seq, seq,
                  pl.BlockSpec((1, ct), lambda c, b: (0, c)),
                  _resident((2 * nb - 1, rows, ct), lambda c, b: (0, 0, c)),
                  _resident((rows, m), lambda c, b: (0, 0)),
                  _resident((m, 2 * m), lambda c, b: (0, 0))],
        out_specs=seq,
        out_shape=jax.ShapeDtypeStruct((batch * n, d), BF16),
        compiler_params=_cparams(2),
        name="hyena_conv",
    )(x0, v, hy_bias.reshape(1, d), g, ma, minv)


def _rope_tables(n):
    axis = ROPE_DIM // 2
    half = axis // 2
    inv = ROPE_THETA ** (-np.arange(0, axis, 2) / axis)
    t = np.arange(n)
    row, col = t // GRID_W, t % GRID_W
    cos = np.ones((n, HEAD_SLOT))
    sin_a = np.zeros((n, HEAD_SLOT))
    sin_b = np.zeros((n, HEAD_SLOT))
    for k, pos in enumerate((row, col)):
        ang = pos[:, None] * inv[None, :]
        base = QK_NOPE + k * axis
        cos[:, base:base + half] = np.cos(ang)
        cos[:, base + half:base + axis] = np.cos(ang)
        sin_a[:, base:base + half] = -np.sin(ang)
        sin_b[:, base + half:base + axis] = np.sin(ang)
    return jnp.asarray(cos.T, F32), jnp.asarray(sin_a.T, F32), jnp.asarray(sin_b.T, F32)


def _expand_keys(ckv_b, kpe_slot, wukv_ref, kng, k_ref, v_ref):
    kw = N_HEADS * HEAD_SLOT
    kv = _dot(ckv_b, wukv_ref[...])
    for h in range(N_HEADS):
        sl = slice(h * HEAD_SLOT, (h + 1) * HEAD_SLOT)
        x = kv[:, sl] + kpe_slot
        scale = lax.rsqrt(jnp.sum(x * x, axis=-1, keepdims=True) * (1.0 / QK_DIM) + EPS)
        k_ref[:, sl] = (x * scale * kng).astype(BF16)
    v_ref[...] = kv[:, kw:].T.astype(BF16)


def _rmsnorm_rows(x, g_col):
    ms = jnp.mean(x * x, axis=0, keepdims=True)
    return x * lax.rsqrt(ms + EPS) * g_col


def _rope_gains(g3, rope):
    if rope is None:
        return g3[:, 0:1], None, None
    cos, sin_a, sin_b = rope
    return g3[:, 0:1] * cos, g3[:, 1:2] * sin_a, g3[:, 2:3] * sin_b


def _partner_term(x, ga, gb):
    half = ROPE_DIM // 4
    return pltpu.roll(x, HEAD_SLOT - half, 0) * ga + pltpu.roll(x, half, 0) * gb


def _slab_norm_rope(x, gc, partner):
    scale = lax.rsqrt(jnp.sum(x * x, axis=0, keepdims=True) * (1.0 / QK_DIM) + EPS)
    y = x * gc
    if partner is not None:
        y = y + partner
    return y * scale


def _mla_prep_kernel(*refs, row_base, tpb, use_rope, emit_latent):
    (x_ref, mod_ref, g_ref, wcat_ref, qg_ref, kvg_ref, wuq_ref, wukv_ref, qng_ref, kng_ref), refs = refs[:10], refs[10:]
    if use_rope:
        (cos_ref, sa_ref, sb_ref), refs = refs[:3], refs[3:]
        rope = (cos_ref[...], sa_ref[...], sb_ref[...])
    else:
        rope = None
    q_ref, k_ref, v_ref = refs[:3]
    i = pl.program_id(0)
    d = x_ref.shape[1]
    r = _mod_row(i, row_base, tpb)
    h = _modnorm(x_ref[...], g_ref[...], _mod_chunk(mod_ref, r, SC1, d), _mod_chunk(mod_ref, r, SH1, d))
    at = _dot(wcat_ref[...], h.T.astype(BF16))
    q_rank = qg_ref.shape[0]
    kv_rank = kvg_ref.shape[0]
    qct = _rmsnorm_rows(at[:q_rank], qg_ref[...]).astype(BF16)
    ckvt = _rmsnorm_rows(at[q_rank:q_rank + kv_rank], kvg_ref[...])
    kpet = at[q_rank + kv_rank:]
    if emit_latent:
        refs[3][...] = ckvt.T
        refs[4][...] = kpet.T
    qt = _dot(wuq_ref[...], qct)
    qc, qa, qb = _rope_gains(qng_ref[...] * (QK_DIM ** -0.5 * math.log2(math.e)), rope)
    for hd in range(N_HEADS):
        sl = slice(hd * HEAD_SLOT, (hd + 1) * HEAD_SLOT)
        x = qt[sl]
        q_ref[sl, :] = _slab_norm_rope(x, qc, _partner_term(x, qa, qb) if use_rope else None).astype(BF16)
    kw = N_HEADS * HEAD_SLOT
    kvt = _dot(wukv_ref[...], ckvt.astype(BF16))
    v_ref[...] = kvt[kw:].astype(BF16)
    kc, ka, kb = _rope_gains(kng_ref[...], rope)
    partner = _partner_term(kpet, ka, kb) if use_rope else None
    for hd in range(N_HEADS):
        sl = slice(hd * HEAD_SLOT, (hd + 1) * HEAD_SLOT)
        k_ref[:, sl] = _slab_norm_rope(kvt[sl] + kpet, kc, partner).T.astype(BF16)


def _mla_prep(x, mod, g, wcat_t, qg, kvg, wuq_t, wukv_t, qng3, kng3, rope_t, *, row_base, rows_per_batch,
              emit_latent):
    n, d = x.shape
    seq = n if rows_per_batch is None else rows_per_batch
    tm = min(512, seq)
    tpb = None if rows_per_batch is None else rows_per_batch // tm
    kw = N_HEADS * HEAD_SLOT
    vw = N_HEADS * V_DIM
    q_rank, kv_rank = qg.shape[0], kvg.shape[0]
    full = lambda a: pl.BlockSpec(a.shape, lambda i: (0,) * a.ndim)
    args = [x, mod, g.reshape(1, d), wcat_t, qg.reshape(-1, 1), kvg.reshape(-1, 1), wuq_t, wukv_t, qng3, kng3]
    in_specs = [pl.BlockSpec((tm, d), lambda i: (i, 0))] + [full(a) for a in args[1:]]
    if rope_t is not None:
        pt = rows_per_batch // tm
        args += list(rope_t)
        in_specs += [pl.BlockSpec((HEAD_SLOT, tm), lambda i: (0, i % pt))] * 3
    row = lambda w: pl.BlockSpec((tm, w), lambda i: (i, 0))
    col = lambda w: pl.BlockSpec((w, tm), lambda i: (0, i))
    out_specs = [col(kw), row(kw), col(vw)]
    out_shape = [jax.ShapeDtypeStruct((kw, n), BF16), jax.ShapeDtypeStruct((n, kw), BF16),
                 jax.ShapeDtypeStruct((vw, n), BF16)]
    if emit_latent:
        out_specs += [row(kv_rank), row(HEAD_SLOT)]
        out_shape += [jax.ShapeDtypeStruct((n, kv_rank), F32), jax.ShapeDtypeStruct((n, HEAD_SLOT), F32)]
    return pl.pallas_call(
        functools.partial(_mla_prep_kernel, row_base=row_base, tpb=tpb, use_rope=rope_t is not None,
                          emit_latent=emit_latent),
        grid=(n // tm,),
        in_specs=in_specs, out_specs=out_specs, out_shape=out_shape,
        compiler_params=_cparams(1),
        name="mla_prep",
    )(*args)


def _mla_cache_kernel(ckv_ref, kpe_ref, wukv_ref, kng_ref, k_ref, v_ref):
    _expand_keys(ckv_ref[...].astype(BF16), kpe_ref[...], wukv_ref, kng_ref[...], k_ref, v_ref)


def _mla_cache(ckv, kpe_slot, wukv, kng):
    n = ckv.shape[0]
    tm = min(512, n)
    kw = N_HEADS * HEAD_SLOT
    vw = N_HEADS * V_DIM
    row = lambda w: pl.BlockSpec((tm, w), lambda i: (i, 0))
    return pl.pallas_call(
        _mla_cache_kernel,
        grid=(n // tm,),
        in_specs=[row(ckv.shape[1]), row(HEAD_SLOT), pl.BlockSpec(wukv.shape, lambda i: (0, 0)),
                  pl.BlockSpec(kng.shape, lambda i: (0, 0))],
        out_specs=[row(kw), pl.BlockSpec((vw, tm), lambda i: (0, i))],
        out_shape=[jax.ShapeDtypeStruct((n, kw), BF16), jax.ShapeDtypeStruct((vw, n), BF16)],
        compiler_params=_cparams(1),
        name="mla_cache_keys",
    )(ckv, kpe_slot, wukv, kng)


KEY_CHUNK = 256


def _attn_kernel(*refs, has_cache, heads):
    if has_cache:
        q_ref, k_ref, v_ref, kc_ref, vc_ref, o_ref = refs
    else:
        q_ref, k_ref, v_ref, o_ref = refs
    n = k_ref.shape[0]
    outs = []
    for hh in range(heads):
        sl = slice(hh * HEAD_SLOT, (hh + 1) * HEAD_SLOT)
        vs = slice(hh * V_DIM, (hh + 1) * V_DIM)
        q = q_ref[sl, :]
        pieces = [(k_ref, v_ref, c, min(KEY_CHUNK, n - c)) for c in range(0, n, KEY_CHUNK)]
        if has_cache:
            pieces.append((kc_ref, vc_ref, 0, kc_ref.shape[0]))
        sts = [_dot(kr[c:c + w, sl], q) for kr, _, c, w in pieces]
        mx = functools.reduce(jnp.maximum, [jnp.max(st, axis=0, keepdims=True) for st in sts])
        pts = [jnp.exp2(st - mx) for st in sts]
        l = functools.reduce(jnp.add, [jnp.sum(pt, axis=0, keepdims=True) for pt in pts])
        ot = functools.reduce(jnp.add, [_dot(vr[vs, c:c + w], pt.astype(BF16))
                                        for (_, vr, c, w), pt in zip(pieces, pts)])
        outs.append(ot / l)
    o_ref[...] = jnp.concatenate(outs, axis=0).T.astype(o_ref.dtype)


def _attention(q, k, v, kc, vc, *, batch, n, past):
    has_cache = kc is not None
    tq = min(512, n)
    heads = 2 if n > tq else N_HEADS
    nq = n // tq
    groups = N_HEADS // heads
    in_specs = [pl.BlockSpec((heads * HEAD_SLOT, tq), lambda b, j, i: (j, b * nq + i)),
                pl.BlockSpec((n, heads * HEAD_SLOT), lambda b, j, i: (b, j)),
                pl.BlockSpec((heads * V_DIM, n), lambda b, j, i: (j, b))]
    args = [q, k, v]
    if has_cache:
        in_specs += [pl.BlockSpec((past, heads * HEAD_SLOT), lambda b, j, i: (b, j)),
                     pl.BlockSpec((heads * V_DIM, past), lambda b, j, i: (j, b))]
        args += [kc, vc]
    return pl.pallas_call(
        functools.partial(_attn_kernel, has_cache=has_cache, heads=heads),
        grid=(batch, groups, nq),
        in_specs=in_specs,
        out_specs=pl.BlockSpec((tq, heads * V_DIM), lambda b, j, i: (b * nq + i, j)),
        out_shape=jax.ShapeDtypeStruct((batch * n, N_HEADS * V_DIM), BF16),
        compiler_params=_cparams(3),
        name="attention",
    )(*args)


def _pad_heads(w, width):
    k = w.shape[0]
    w = w.reshape(k, N_HEADS, width)
    return jnp.pad(w, ((0, 0), (0, 0), (0, HEAD_SLOT - width))).reshape(k, N_HEADS * HEAD_SLOT)


def _slot_gain(g):
    return jnp.pad(g, (0, HEAD_SLOT - QK_DIM)).reshape(1, HEAD_SLOT)


def _gain_columns(g_row):
    g = g_row.reshape(HEAD_SLOT)
    half = ROPE_DIM // 4
    return jnp.stack([g, jnp.roll(g, -half), jnp.roll(g, half)], axis=1)


def kernel(x_prompt, x_sample, cache_ckv, cache_kpe, c, c_ctx, norm1_g, norm2_g, mod_w, mod_b, mlp_w1, mlp_w2, sc_w_in, sc_conv_w, sc_conv_b, sc_w_out, mla_w_dq, mla_q_norm_g, mla_w_uq, mla_w_dkv, mla_kv_norm_g, mla_w_ukv, mla_qn_g, mla_kn_g, mla_w_o, hy_w_in, hy_conv_w, hy_conv_b, hy_f_w0, hy_f_b0, hy_f_w1, hy_f_b1, hy_f_w2, hy_f_b2, hy_f_w3, hy_sin_freq, hy_bias, hy_w_out):
    batch, seq, d = x_prompt.shape
    dec_batch, dec_seq, _ = x_sample.shape
    depth = mod_w.shape[0]
    n_mla = mla_w_dq.shape[0]
    past = cache_ckv.shape[2]
    kv_rank = mla_kv_norm_g.shape[1]
    assert 1 + dec_batch <= MOD_ROWS

    cond = jnp.zeros((MOD_ROWS, d), F32).at[0].set(c_ctx).at[1:1 + dec_batch].set(c)
    mods = _modulation(cond, mod_w, mod_b)

    w1 = mlp_w1.astype(BF16)
    w2 = mlp_w2.astype(BF16)
    sc_in = sc_w_in.astype(BF16)
    sc_out = sc_w_out.astype(BF16)
    hy_in = hy_w_in.astype(BF16)
    hy_out = hy_w_out.astype(BF16)
    mla_o = mla_w_o.astype(BF16)

    streams = [
        dict(x=x_prompt.reshape(batch * seq, d), row_base=0, rpb=None, seq=seq, nseq=batch, ctx=True),
        dict(x=x_sample.reshape(dec_batch * dec_seq, d), row_base=1, rpb=dec_seq, seq=dec_seq,
             nseq=dec_batch, ctx=False),
    ]
    hy_consts = {}
    new_ckv, new_kpe = [], []

    for i in range(depth):
        kind, j = i % 3, i // 3
        mod = mods[i]
        for st in streams:
            x = st["x"]
            rb, rpb, n = st["row_base"], st["rpb"], st["seq"]
            tail = None
            if kind == 0:
                x = _sconv(x, mod, norm1_g[i], sc_in, sc_conv_w[j], sc_conv_b[j], sc_out, j, seq_len=n,
                           row_base=rb, rows_per_batch=rpb)
            elif kind == 1:
                pe = mla_w_dkv[j][:, kv_rank:]
                pe_slot = jnp.pad(pe, ((0, 0), (QK_NOPE, HEAD_SLOT - QK_DIM)))
                wcat = jnp.concatenate([mla_w_dq[j], mla_w_dkv[j][:, :kv_rank], pe_slot], axis=1).astype(BF16)
                wuq = _pad_heads(mla_w_uq[j], QK_DIM).astype(BF16)
                ukv = mla_w_ukv[j].reshape(kv_rank, N_HEADS, QK_NOPE + V_DIM)
                wukv = jnp.concatenate(
                    [_pad_heads(ukv[:, :, :QK_NOPE].reshape(kv_rank, -1), QK_NOPE),
                     ukv[:, :, QK_NOPE:].reshape(kv_rank, -1)], axis=1).astype(BF16)
                qng, kng = _slot_gain(mla_qn_g[j]), _slot_gain(mla_kn_g[j])
                rope = None if st["ctx"] else _rope_tables(n)
                outs = _mla_prep(x, mod, norm1_g[i], wcat.T, mla_q_norm_g[j], mla_kv_norm_g[j], wuq.T, wukv.T,
                                 _gain_columns(qng), _gain_columns(kng), rope, row_base=rb,
                                 rows_per_batch=rpb, emit_latent=st["ctx"])
                q, k, v = outs[:3]
                if st["ctx"]:
                    new_ckv.append(outs[3].reshape(batch, seq, kv_rank))
                    new_kpe.append(outs[4][:, QK_NOPE:QK_DIM].reshape(batch, seq, ROPE_DIM))
                    kc = vc = None
                else:
                    ck = cache_ckv[:, j].reshape(dec_batch * past, kv_rank)
                    cp = jnp.pad(cache_kpe[:, j].reshape(dec_batch * past, ROPE_DIM),
                                 ((0, 0), (QK_NOPE, HEAD_SLOT - QK_DIM)))
                    kc, vc = _mla_cache(ck, cp, wukv, kng)
                o = _attention(q, k, v, kc, vc, batch=st["nseq"], n=n, past=past)
                tail = (o, mla_o, j)
            else:
                m = min(HY_BLOCK, n)
                if m not in hy_consts:
                    hy_consts[m] = _dft_constants(m)
                mats, minv = hy_consts[m]
                g = _hy_filter(n, m, hy_f_w0[j], hy_f_b0[j], hy_f_w1[j], hy_f_b1[j], hy_f_w2[j],
                               hy_f_b2[j], hy_f_w3[j], hy_sin_freq[j], mats)
                x0, v1 = _hyena_in(x, mod, norm1_g[i], hy_in, j, hy_conv_w[j], hy_conv_b[j], seq_len=n,
                                   row_base=rb, rows_per_batch=rpb)
                z = _hy_conv(x0, v1, hy_bias[j], g, mats[FWD0], minv, n=n, m=m, batch=st["nseq"])
                tail = (z, hy_out, j)
            x = _mlp(x, mod, norm2_g[i], w1, w2, i, row_base=rb, rows_per_batch=rpb, tail=tail)
            st["x"] = x

    y_prompt = streams[0]["x"].reshape(batch, seq, d)
    y_sample = streams[1]["x"].reshape(dec_batch, dec_seq, d)
    return (y_prompt, y_sample, jnp.stack(new_ckv, axis=1), jnp.stack(new_kpe, axis=1))
```
